```python
import jax, jax.numpy as jnp
from jax import lax
import numpy as np

D_MODEL = 1024
BATCH = 2
SEQ = 8192
DEPTH = 1
DEC_BATCH = 32
DEC_SEQ = 64
PAST_LEN = 4096

CHUNK = 64
CONV_WIDTH = 31
CONV_BUF = CONV_WIDTH - 1
E_CONV = D_MODEL
N_HEADS = 4
HEAD_DIM = 256
E_MLSTM = N_HEADS * HEAD_DIM
D_FF = 4 * D_MODEL
D_IN = 2 * E_CONV + 4 * E_MLSTM + 2 * N_HEADS + 2 * D_MODEL
ALPHA = (2.0 * DEPTH) ** 0.25
BETA = (8.0 * DEPTH) ** -0.25
LN_EPS = 1e-5

kernel_name = "hybrid_conformer_mlstm_stream_step"


def layer_norm(x, g, b):
    xf = x.astype(jnp.float32)
    mu = jnp.mean(xf, axis=-1, keepdims=True)
    var = jnp.mean(jnp.square(xf - mu), axis=-1, keepdims=True)
    y = (xf - mu) * lax.rsqrt(var + LN_EPS)
    return (y * g.astype(jnp.float32) + b.astype(jnp.float32)).astype(x.dtype)


def mlstm_chunk(carry, blk):
    C0, n0, m0 = carry
    q, k, v, ig, lf = blk
    L = q.shape[2]
    b = jnp.cumsum(lf, axis=-1)
    a = b + m0[..., None]
    causal = jnp.tril(jnp.ones((L, L), dtype=bool))
    d = jnp.where(causal, b[..., :, None] - b[..., None, :] + ig[..., None, :], -jnp.inf)
    m = jnp.maximum(a, jnp.max(d, axis=-1))
    w_intra = jnp.exp(d - m[..., None])
    w_inter = jnp.exp(a - m)
    s = jnp.einsum('bhtk,bhsk->bhts', q, k) * w_intra
    num = jnp.einsum('bhts,bhsv->bhtv', s, v) + w_inter[..., None] * jnp.einsum('bhtk,bhkv->bhtv', q, C0)
    den = jnp.sum(s, axis=-1) + w_inter * jnp.einsum('bhtk,bhk->bht', q, n0)
    h = num / jnp.maximum(jnp.abs(den), jnp.exp(-m))[..., None]
    m_end = m[..., -1]
    w_end = jnp.exp(b[..., -1:] - b + ig - m_end[..., None])
    decay = jnp.exp(a[..., -1] - m_end)
    C1 = decay[..., None, None] * C0 + jnp.einsum('bhs,bhsk,bhsv->bhkv', w_end, k, v)
    n1 = decay[..., None] * n0 + jnp.einsum('bhs,bhsk->bhk', w_end, k)
    return (C1, n1, m_end), h


def mlstm(q, k, v, ig, lf, C0, n0, m0):
    Bsz, S = q.shape[0], q.shape[1]
    L = min(S, CHUNK)
    nc = S // L
    def blocks4(t):
        return t.reshape(Bsz, nc, L, N_HEADS, HEAD_DIM).transpose(1, 0, 3, 2, 4)
    def blocks3(t):
        return t.reshape(Bsz, nc, L, N_HEADS).transpose(1, 0, 3, 2)
    (C1, n1, m1), h = lax.scan(mlstm_chunk, (C0, n0, m0),
                               (blocks4(q), blocks4(k), blocks4(v), blocks3(ig), blocks3(lf)))
    h = h.transpose(1, 0, 3, 2, 4).reshape(Bsz, S, N_HEADS, HEAD_DIM)
    return h, C1, n1, m1


def encoder_layer(x, conv_buf, C0, n0, m0, w_in, b_gate, w_dw, b_dw, ln_a_g, ln_a_b, w_a_out,
                  hn_g, w_b_out, w_out, ln1_g, ln1_b, w_ff1, w_ff2, ln2_g, ln2_b):
    Bsz, S, _ = x.shape
    z = x @ w_in
    o1 = 2 * E_CONV
    o2 = o1 + E_MLSTM
    o3 = o2 + E_MLSTM
    o4 = o3 + E_MLSTM
    o5 = o4 + E_MLSTM
    o6 = o5 + N_HEADS
    o7 = o6 + N_HEADS
    o8 = o7 + D_MODEL
    z_conv, z_q, z_k, z_v, z_o = z[..., :o1], z[..., o1:o2], z[..., o2:o3], z[..., o3:o4], z[..., o4:o5]
    z_i, z_f, g_a, g_b = z[..., o5:o6], z[..., o6:o7], z[..., o7:o8], z[..., o8:]

    glu = z_conv[..., :E_CONV] * jax.nn.sigmoid(z_conv[..., E_CONV:])
    conv_in = jnp.concatenate([conv_buf.astype(glu.dtype), glu], axis=1)
    dw = lax.conv_general_dilated(conv_in, w_dw[:, None, :].astype(conv_in.dtype), window_strides=(1,),
                                  padding='VALID', dimension_numbers=('NWC', 'WIO', 'NWC'),
                                  feature_group_count=E_CONV) + b_dw
    y_a = jax.nn.silu(layer_norm(dw, ln_a_g, ln_a_b)) @ w_a_out
    new_buf = conv_in[:, -CONV_BUF:]

    def heads(t):
        return t.reshape(Bsz, S, N_HEADS, HEAD_DIM).astype(jnp.float32)
    ig = (z_i + b_gate[:N_HEADS]).astype(jnp.float32)
    lf = jax.nn.log_sigmoid((z_f + b_gate[N_HEADS:]).astype(jnp.float32))
    h, C1, n1, m1 = mlstm(heads(z_q), heads(z_k) * (HEAD_DIM ** -0.5), heads(z_v), ig, lf, C0, n0, m0)
    mu = jnp.mean(h, axis=-1, keepdims=True)
    var = jnp.mean(jnp.square(h - mu), axis=-1, keepdims=True)
    hn = (h - mu) * lax.rsqrt(var + LN_EPS) * hn_g.reshape(N_HEADS, HEAD_DIM).astype(jnp.float32)
    y_b = (jax.nn.sigmoid(z_o) * hn.reshape(Bsz, S, E_MLSTM).astype(x.dtype)) @ w_b_out

    merged = jax.nn.sigmoid(g_a) * y_a + jax.nn.sigmoid(g_b) * y_b
    x1 = layer_norm(ALPHA * x + merged @ w_out, ln1_g, ln1_b)
    ff = jnp.square(jax.nn.relu(x1 @ w_ff1)) @ w_ff2
    x2 = layer_norm(ALPHA * x1 + ff, ln2_g, ln2_b)
    return x2, new_buf, C1, n1, m1


def setup_inputs(seed: int = 0) -> dict:
    key = jax.random.key(seed)
    ks = jax.random.split(key, 24)
    nrm = jax.random.normal
    f32 = jnp.float32
    x_prompt = nrm(ks[0], (BATCH, SEQ, D_MODEL), f32)
    x_sample = nrm(ks[1], (DEC_BATCH, DEC_SEQ, D_MODEL), f32)
    cache_conv = 0.5 * nrm(ks[2], (DEPTH, DEC_BATCH, CONV_BUF, E_CONV), f32)
    state_C = 0.05 * nrm(ks[3], (DEPTH, DEC_BATCH, N_HEADS, HEAD_DIM, HEAD_DIM), f32)
    state_n = 0.05 * nrm(ks[4], (DEPTH, DEC_BATCH, N_HEADS, HEAD_DIM), f32)
    state_m = 0.5 * nrm(ks[5], (DEPTH, DEC_BATCH, N_HEADS), f32)
    w_in = nrm(ks[6], (DEPTH, D_MODEL, D_IN), f32) * D_MODEL ** -0.5
    b_gate = jnp.concatenate([0.1 * nrm(ks[7], (DEPTH, N_HEADS), f32),
                              3.0 + 0.5 * nrm(ks[8], (DEPTH, N_HEADS), f32)], axis=-1)
    w_dw = nrm(ks[9], (DEPTH, CONV_WIDTH, E_CONV), f32) * CONV_WIDTH ** -0.5
    b_dw = 0.02 * nrm(ks[10], (DEPTH, E_CONV), f32)
    ln_a_g = 1.0 + 0.02 * nrm(ks[11], (DEPTH, E_CONV), f32)
    ln_a_b = 0.02 * nrm(ks[12], (DEPTH, E_CONV), f32)
    w_a_out = nrm(ks[13], (DEPTH, E_CONV, D_MODEL), f32) * (E_CONV ** -0.5) * BETA
    hn_g = 1.0 + 0.02 * nrm(ks[14], (DEPTH, E_MLSTM), f32)
    w_b_out = nrm(ks[15], (DEPTH, E_MLSTM, D_MODEL), f32) * (E_MLSTM ** -0.5) * BETA
    w_out = nrm(ks[16], (DEPTH, D_MODEL, D_MODEL), f32) * (D_MODEL ** -0.5) * BETA
    ln1_g = 1.0 + 0.02 * nrm(ks[17], (DEPTH, D_MODEL), f32)
    ln1_b = 0.02 * nrm(ks[18], (DEPTH, D_MODEL), f32)
    w_ff1 = nrm(ks[19], (DEPTH, D_MODEL, D_FF), f32) * D_MODEL ** -0.5
    w_ff2 = nrm(ks[20], (DEPTH, D_FF, D_MODEL), f32) * (D_FF ** -0.5) * BETA
    ln2_g = 1.0 + 0.02 * nrm(ks[21], (DEPTH, D_MODEL), f32)
    ln2_b = 0.02 * nrm(ks[22], (DEPTH, D_MODEL), f32)
    return {"x_prompt": x_prompt, "x_sample": x_sample, "cache_conv": cache_conv,
            "state_C": state_C, "state_n": state_n, "state_m": state_m,
            "w_in": w_in, "b_gate": b_gate, "w_dw": w_dw, "b_dw": b_dw,
            "ln_a_g": ln_a_g, "ln_a_b": ln_a_b, "w_a_out": w_a_out, "hn_g": hn_g,
            "w_b_out": w_b_out, "w_out": w_out, "ln1_g": ln1_g, "ln1_b": ln1_b,
            "w_ff1": w_ff1, "w_ff2": w_ff2, "ln2_g": ln2_g, "ln2_b": ln2_b}


def reference(x_prompt, x_sample, cache_conv, state_C, state_n, state_m, w_in, b_gate, w_dw, b_dw,
              ln_a_g, ln_a_b, w_a_out, hn_g, w_b_out, w_out, ln1_g, ln1_b, w_ff1, w_ff2, ln2_g, ln2_b):
    yp, ys = x_prompt, x_sample
    bp = x_prompt.shape[0]
    pc, pC, pn, pm = [], [], [], []
    sc, sC, sn, sm = [], [], [], []
    for l in range(DEPTH):
        params = (w_in[l], b_gate[l], w_dw[l], b_dw[l], ln_a_g[l], ln_a_b[l], w_a_out[l], hn_g[l],
                  w_b_out[l], w_out[l], ln1_g[l], ln1_b[l], w_ff1[l], w_ff2[l], ln2_g[l], ln2_b[l])
        buf0 = jnp.zeros((bp, CONV_BUF, E_CONV), x_prompt.dtype)
        C0 = jnp.zeros((bp, N_HEADS, HEAD_DIM, HEAD_DIM), jnp.float32)
        n0 = jnp.zeros((bp, N_HEADS, HEAD_DIM), jnp.float32)
        m0 = jnp.zeros((bp, N_HEADS), jnp.float32)
        yp, b1, C1, n1, m1 = encoder_layer(yp, buf0, C0, n0, m0, *params)
        pc.append(b1); pC.append(C1); pn.append(n1); pm.append(m1)
        ys, b2, C2, n2, m2 = encoder_layer(ys, cache_conv[l], state_C[l].astype(jnp.float32),
                                           state_n[l].astype(jnp.float32),
                                           state_m[l].astype(jnp.float32), *params)
        sc.append(b2); sC.append(C2); sn.append(n2); sm.append(m2)
    return (yp, ys, jnp.stack(pc), jnp.stack(pC), jnp.stack(pn), jnp.stack(pm),
            jnp.stack(sc), jnp.stack(sC), jnp.stack(sn), jnp.stack(sm))
```

```python
import functools

import jax
import jax.numpy as jnp
from jax import lax
from jax.experimental import pallas as pl
from jax.experimental.pallas import tpu as pltpu

D_MODEL = 1024
N_HEADS = 4
HEAD_DIM = 256
CONV_WIDTH = 31
CONV_BUF = CONV_WIDTH - 1
D_FF = 4 * D_MODEL
DEPTH = 1
ALPHA = (2.0 * DEPTH) ** 0.25
LN_EPS = 1e-5

LANES = 128
SUBLANES = 8
HIST_ROWS = 32
CONV_ROW_BLOCK = 64
MASKED = -1e30
VMEM_LIMIT_BYTES = 58 * 1024 * 1024

F32 = jnp.float32
BF16 = jnp.bfloat16


def _mm(a, b):
    return jnp.dot(a, b, preferred_element_type=F32)


def _mm_nt(a, b):
    return lax.dot_general(a, b, (((1,), (1,)), ((), ())), preferred_element_type=F32)


def _mm_tn(a, b):
    return lax.dot_general(a, b, (((0,), (0,)), ((), ())), preferred_element_type=F32)


def _sigmoid(x):
    return 0.5 * (jnp.tanh(0.5 * x) + 1.0)


def _log_sigmoid(x):
    return jnp.minimum(x, 0.0) - jnp.log(1.0 + jnp.exp(-jnp.abs(x)))


def _layer_norm(x, g, b):
    mu = jnp.mean(x, axis=-1, keepdims=True)
    xc = x - mu
    var = jnp.mean(xc * xc, axis=-1, keepdims=True)
    return xc * lax.rsqrt(var + LN_EPS) * g + b


def _mixer_kernel(ns, L, nj,
                  x_ref, cache_ref, c_in, n_in, m_in,
                  wconv, wqkvo, wg, wgc, wgri, wgrf, bgc, bgr, wdw, bdw, lnag, lnab,
                  waout, hng, wbout, wout, ln1g, ln1b,
                  x1_ref, conv_out, c_out, n_out, m_out,
                  convbuf, ybuf):
    j = pl.program_id(1)
    carried = nj > 1
    last = nj - 1
    lo = HIST_ROWS - CONV_BUF

    if carried:
        @pl.when(j == 0)
        def _():
            c_out[...] = c_in[...]
            n_out[...] = n_in[...]
            m_out[...] = m_in[...]
            for s in range(ns):
                convbuf[s, lo:HIST_ROWS, :] = cache_ref[s]
        c_prev, n_prev, m_prev = c_out, n_out, m_out
    else:
        for s in range(ns):
            convbuf[s, lo:HIST_ROWS, :] = cache_ref[s]
        c_prev, n_prev, m_prev = c_in, n_in, m_in

    x = x_ref[...]
    xb = x.astype(BF16)

    zc = _mm(xb, wconv[...])
    glu = zc[:, :D_MODEL] * _sigmoid(zc[:, D_MODEL:])
    for s in range(ns):
        convbuf[s, HIST_ROWS:HIST_ROWS + L, :] = glu[s * L:(s + 1) * L]

    rb = min(CONV_ROW_BLOCK, L)
    dw_rows = []
    for s in range(ns):
        for r0 in range(0, L, rb):
            cols = []
            for c0 in range(0, D_MODEL, LANES):
                acc = jnp.broadcast_to(bdw[:, c0:c0 + LANES], (rb, LANES))
                for t in range(CONV_WIDTH):
                    acc = acc + (convbuf[s, lo + r0 + t:lo + r0 + t + rb, c0:c0 + LANES]
                                 * wdw[t:t + 1, c0:c0 + LANES])
                cols.append(acc)
            dw_rows.append(jnp.concatenate(cols, axis=1))
    dw = jnp.concatenate(dw_rows, axis=0) if len(dw_rows) > 1 else dw_rows[0]

    for s in range(ns):
        tail = convbuf[s, HIST_ROWS + L - CONV_BUF:HIST_ROWS + L, :]
        if carried:
            convbuf[s, lo:HIST_ROWS, :] = tail

            @pl.when(j == last)
            def _():
                conv_out[s] = tail
        else:
            conv_out[s] = tail

    ya_in = _layer_norm(dw, lnag[...], lnab[...])
    ya_in = ya_in * _sigmoid(ya_in)
    y_a = _mm(ya_in.astype(BF16), waout[...])

    zq = _mm(xb, wqkvo[...])
    zgc = _mm(xb, wgc[...]) + bgc[...]
    ig_col = zgc[:, :LANES]
    lf_col = _log_sigmoid(zgc[:, LANES:])
    ig_row = _mm_nt(wgri[...], xb) + bgr[:, 0:1]
    lf_row = _log_sigmoid(_mm_nt(wgrf[...], xb) + bgr[:, 1:2])

    rows = lax.broadcasted_iota(jnp.int32, (L, L), 0)
    cols_i = lax.broadcasted_iota(jnp.int32, (L, L), 1)
    causal = rows >= cols_i
    tri_l = causal.astype(F32)
    tri_u = (rows <= cols_i).astype(F32)
    lane = lax.broadcasted_iota(jnp.int32, (1, LANES), 1)
    scale = HEAD_DIM ** -0.5

    for s in range(ns):
        sl = slice(s * L, (s + 1) * L)
        b_col = jnp.dot(tri_l, lf_col[sl], preferred_element_type=F32,
                        precision=lax.Precision.HIGHEST)
        b_row = jnp.dot(lf_row[:, sl], tri_u, preferred_element_type=F32,
                        precision=lax.Precision.HIGHEST)
        m_old = m_prev[s]
        m_new = m_old
        for h in range(N_HEADS):
            hs = slice(h * HEAD_DIM, (h + 1) * HEAD_DIM)
            b_c = b_col[:, h:h + 1]
            ig_c = ig_col[sl, h:h + 1]
            b_r = b_row[h:h + 1, :]
            ig_r = ig_row[h:h + 1, sl]
            m0 = m_old[:, h:h + 1]
            a_c = b_c + m0
            dm = jnp.where(causal, b_c + (ig_r - b_r), MASKED)
            m_c = jnp.maximum(a_c, jnp.max(dm, axis=1, keepdims=True))
            w_intra = jnp.exp(dm - m_c)
            w_inter = jnp.exp(a_c - m_c)

            q = zq[sl, h * HEAD_DIM:(h + 1) * HEAD_DIM]
            k = zq[sl, D_MODEL + h * HEAD_DIM:D_MODEL + (h + 1) * HEAD_DIM] * scale
            v = zq[sl, 2 * D_MODEL + h * HEAD_DIM:2 * D_MODEL + (h + 1) * HEAD_DIM]
            o = zq[sl, 3 * D_MODEL + h * HEAD_DIM:3 * D_MODEL + (h + 1) * HEAD_DIM]
            qb = q.astype(BF16)
            vb = v.astype(BF16)
            c0 = c_prev[s, h]
            n0 = n_prev[s, h]

            smat = _mm_nt(qb, k.astype(BF16)) * w_intra
            num = _mm(smat.astype(BF16), vb) + w_inter * _mm(qb, c0.astype(BF16))
            den = (jnp.sum(smat, axis=1, keepdims=True)
                   + w_inter * jnp.sum(q * n0, axis=1, keepdims=True))
            hh = num / jnp.maximum(jnp.abs(den), jnp.exp(-m_c))

            m_end = m_c[L - 1:L]
            w_end = jnp.exp(b_c[L - 1:L] - b_c + ig_c - m_end)
            decay = jnp.exp(a_c[L - 1:L] - m_end)
            kw = k * w_end
            c_out[s, h] = decay * c0 + _mm_tn(kw.astype(BF16), vb)
            n_out[s, h] = decay * n0 + jnp.sum(kw, axis=0, keepdims=True)
            m_new = jnp.where(lane == h, m_end, m_new)

            mu = jnp.mean(hh, axis=1, keepdims=True)
            hc = hh - mu
            var = jnp.mean(hc * hc, axis=1, keepdims=True)
            hn = hc * lax.rsqrt(var + LN_EPS) * hng[:, hs]
            ybuf[sl, hs] = (_sigmoid(o) * hn).astype(BF16)
        m_out[s] = m_new

    y_b = _mm(ybuf[...], wbout[...])

    zg = _mm(xb, wg[...])
    merged = _sigmoid(zg[:, :D_MODEL]) * y_a + _sigmoid(zg[:, D_MODEL:]) * y_b
    res = ALPHA * x + _mm(merged.astype(BF16), wout[...])
    x1_ref[...] = _layer_norm(res, ln1g[...], ln1b[...])


def _ffn_kernel(x1_ref, wff1, wff2, ln2g, ln2b, out_ref):
    x1 = x1_ref[...]
    hid = jnp.maximum(_mm(x1.astype(BF16), wff1[...]), 0.0)
    ff = _mm((hid * hid).astype(BF16), wff2[...])
    out_ref[...] = _layer_norm(ALPHA * x1 + ff, ln2g[...], ln2b[...])


def _resident(shape):
    nd = len(shape)
    return pl.BlockSpec(shape, lambda *_: (0,) * nd, pipeline_mode=pl.Buffered(1))


def _mixer(x2d, cache, c0, n0, m0, weights, *, ns, L, nj):
    T = ns * L
    n_tok = x2d.shape[0]
    G = n_tok // (T * nj)
    n_seq = G * ns
    state_specs = [
        pl.BlockSpec((ns, CONV_BUF, D_MODEL), lambda g, j: (g, 0, 0)),
        pl.BlockSpec((ns, N_HEADS, HEAD_DIM, HEAD_DIM), lambda g, j: (g, 0, 0, 0)),
        pl.BlockSpec((ns, N_HEADS, 1, HEAD_DIM), lambda g, j: (g, 0, 0, 0)),
        pl.BlockSpec((ns, 1, LANES), lambda g, j: (g, 0, 0)),
    ]
    x_spec = pl.BlockSpec((T, D_MODEL), lambda g, j: (g * nj + j, 0))
    return pl.pallas_call(
        functools.partial(_mixer_kernel, ns, L, nj),
        grid=(G, nj),
        in_specs=[x_spec] + state_specs + [_resident(w.shape) for w in weights],
        out_specs=[x_spec] + state_specs,
        out_shape=[
            jax.ShapeDtypeStruct((n_tok, D_MODEL), F32),
            jax.ShapeDtypeStruct((n_seq, CONV_BUF, D_MODEL), F32),
            jax.ShapeDtypeStruct((n_seq, N_HEADS, HEAD_DIM, HEAD_DIM), F32),
            jax.ShapeDtypeStruct((n_seq, N_HEADS, 1, HEAD_DIM), F32),
            jax.ShapeDtypeStruct((n_seq, 1, LANES), F32),
        ],
        scratch_shapes=[
            pltpu.VMEM((ns, HIST_ROWS + L, D_MODEL), F32),
            pltpu.VMEM((T, D_MODEL), BF16),
        ],
        compiler_params=pltpu.CompilerParams(
            dimension_semantics=("arbitrary", "arbitrary"),
            vmem_limit_bytes=VMEM_LIMIT_BYTES),
        name=f"mixer_ns{ns}",
    )(x2d, cache, c0, n0, m0, *weights)


def _ffn(x1, wff1, wff2, ln2g, ln2b, *, tile):
    n_tok = x1.shape[0]
    x_spec = pl.BlockSpec((tile, D_MODEL), lambda i: (i, 0))
    weights = (wff1, wff2, ln2g, ln2b)
    return pl.pallas_call(
        _ffn_kernel,
        grid=(n_tok // tile,),
        in_specs=[x_spec] + [_resident(w.shape) for w in weights],
        out_specs=x_spec,
        out_shape=jax.ShapeDtypeStruct((n_tok, D_MODEL), F32),
        compiler_params=pltpu.CompilerParams(
            dimension_semantics=("arbitrary",),
            vmem_limit_bytes=VMEM_LIMIT_BYTES),
        name="ffn",
    )(x1, *weights)


def _pad_m(m):
    return jnp.pad(m.astype(F32), ((0, 0), (0, LANES - N_HEADS)))[:, None, :]


def kernel(x_prompt, x_sample, cache_conv, state_C, state_n, state_m, w_in, b_gate, w_dw, b_dw,
           ln_a_g, ln_a_b, w_a_out, hn_g, w_b_out, w_out, ln1_g, ln1_b, w_ff1, w_ff2, ln2_g, ln2_b):
    assert w_in.shape[0] == DEPTH == 1
    bp, sp, _ = x_prompt.shape
    bs, ss, _ = x_sample.shape
    e = D_MODEL
    wi = w_in[0]
    o_qkvo = 2 * e
    o_gate = o_qkvo + 4 * e
    o_ga = o_gate + 2 * N_HEADS

    def row(v):
        return v[0].astype(F32)[None, :]

    w_i = wi[:, o_gate:o_gate + N_HEADS]
    w_f = wi[:, o_gate + N_HEADS:o_ga]
    zpad = jnp.zeros((e, LANES - N_HEADS), wi.dtype)
    wgc = jnp.concatenate([w_i, zpad, w_f, zpad], axis=1).astype(BF16)
    rpad = jnp.zeros((SUBLANES - N_HEADS, e), wi.dtype)
    wgri = jnp.concatenate([w_i.T, rpad], axis=0).astype(BF16)
    wgrf = jnp.concatenate([w_f.T, rpad], axis=0).astype(BF16)
    bg = b_gate[0].astype(F32)
    bpad = jnp.zeros((LANES - N_HEADS,), F32)
    bgc = jnp.concatenate([bg[:N_HEADS], bpad, bg[N_HEADS:], bpad])[None, :]
    bgr = jnp.pad(bg.reshape(2, N_HEADS).T, ((0, SUBLANES - N_HEADS), (0, 0)))

    weights = (
        wi[:, :o_qkvo].astype(BF16), wi[:, o_qkvo:o_gate].astype(BF16), wi[:, o_ga:].astype(BF16),
        wgc, wgri, wgrf, bgc, bgr,
        w_dw[0].astype(F32), row(b_dw), row(ln_a_g), row(ln_a_b),
        w_a_out[0].astype(BF16), row(hn_g), w_b_out[0].astype(BF16), w_out[0].astype(BF16),
        row(ln1_g), row(ln1_b),
    )
    ffn_w = (w_ff1[0].astype(BF16), w_ff2[0].astype(BF16), row(ln2_g), row(ln2_b))

    Lp = 256
    x1p, convp, cp, np_, mp = _mixer(
        x_prompt.reshape(bp * sp, e),
        jnp.zeros((bp, CONV_BUF, e), F32),
        jnp.zeros((bp, N_HEADS, HEAD_DIM, HEAD_DIM), F32),
        jnp.zeros((bp, N_HEADS, 1, HEAD_DIM), F32),
        jnp.zeros((bp, 1, LANES), F32),
        weights, ns=1, L=Lp, nj=sp // Lp)
    ns_s = 4
    x1s, convs, cs, ns_, ms = _mixer(
        x_sample.reshape(bs * ss, e),
        cache_conv[0].astype(F32),
        state_C[0].astype(F32),
        state_n[0].astype(F32)[:, :, None, :],
        _pad_m(state_m[0]),
        weights, ns=ns_s, L=ss, nj=1)

    yp = _ffn(x1p, *ffn_w, tile=512).reshape(bp, sp, e)
    ys = _ffn(x1s, *ffn_w, tile=512).reshape(bs, ss, e)

    def st(conv, c, n, m):
        return conv[None], c[None], n[:, :, 0, :][None], m[:, 0, :N_HEADS][None]

    return (yp, ys) + st(convp, cp, np_, mp) + st(convs, cs, ns_, ms)
```

```python
import functools

import jax
import jax.numpy as jnp
from jax import lax
from jax.experimental import pallas as pl
from jax.experimental.pallas import tpu as pltpu

D_MODEL = 1024
N_HEADS = 4
HEAD_DIM = 256
CONV_WIDTH = 31
CONV_BUF = CONV_WIDTH - 1
D_FF = 4 * D_MODEL
DEPTH = 1
ALPHA = (2.0 * DEPTH) ** 0.25
LN_EPS = 1e-5

LANES = 128
SUBLANES = 8
MXU_DIM = 256
HIST_ROWS = 32
CONV_ROW_BLOCK = 256
MASKED = -1e30
VMEM_LIMIT_BYTES = 58 * 1024 * 1024

F32 = jnp.float32
BF16 = jnp.bfloat16


def _mm(a, b):
    return jnp.dot(a, b, preferred_element_type=F32)


def _mm_nt(a, b):
    return lax.dot_general(a, b, (((1,), (1,)), ((), ())), preferred_element_type=F32)


def _mm_tn(a, b):
    return lax.dot_general(a, b, (((0,), (0,)), ((), ())), preferred_element_type=F32)


def _mm_f32(a, b):
    return jnp.dot(a, b, preferred_element_type=F32, precision=lax.Precision.HIGHEST)


def _pack_rows(w):
    k, n = w.shape
    pairs = jnp.swapaxes(w.astype(BF16).reshape(k // 2, 2, n), -1, -2)
    return lax.bitcast_convert_type(pairs, jnp.uint32)


def _w(ref, cols=slice(None)):
    return pltpu.bitcast(ref[:, cols], BF16)


def _sigmoid(x):
    return 0.5 * (jnp.tanh(0.5 * x) + 1.0)


def _log_sigmoid(x):
    return jnp.minimum(x, 0.0) - jnp.log(1.0 + jnp.exp(-jnp.abs(x)))


def _layer_norm(x, g, b):
    mu = jnp.mean(x, axis=-1, keepdims=True)
    xc = x - mu
    var = jnp.mean(xc * xc, axis=-1, keepdims=True)
    return xc * lax.rsqrt(var + LN_EPS) * g + b


def _mixer_kernel(ns, nc, L, nj,
                  x_ref, cache_ref, c_in, n_in, m_in,
                  wconv, wqkvo, wg, wgc, wgri, wgrf, bgc, bgr, wdw, bdw, lnag, lnab,
                  waout, hng, wbout, wout, ln1g, ln1b,
                  x1_ref, conv_out, c_out, n_out, m_out,
                  xb_scr, convbuf, dw_scr, yain_scr, zq_scr, zg_scr, ybuf, mrg_scr):
    j = pl.program_id(1)
    carried = nj > 1
    Lc = nc * L
    T = ns * Lc
    lo = HIST_ROWS - CONV_BUF
    quarter = D_MODEL // 4

    n_blk = D_MODEL // LANES

    def load_history():
        for s in range(ns):
            for b in range(n_blk):
                convbuf[s, b, lo:HIST_ROWS, :] = cache_ref[s, :, b * LANES:(b + 1) * LANES]

    if carried:
        @pl.when(j == 0)
        def _():
            c_out[...] = c_in[...]
            n_out[...] = n_in[...]
            m_out[...] = m_in[...]
            load_history()
    else:
        load_history()

    xb_scr[...] = x_ref[...].astype(BF16)

    def glu_quarter(i):
        cs = slice(i * quarter, (i + 1) * quarter)
        gs = slice(D_MODEL + i * quarter, D_MODEL + (i + 1) * quarter)
        xb = xb_scr[...]
        glu = _mm(xb, _w(wconv, cs)) * _sigmoid(_mm(xb, _w(wconv, gs)))
        for s in range(ns):
            for b in range(quarter // LANES):
                convbuf[s, i * (quarter // LANES) + b, HIST_ROWS:HIST_ROWS + Lc, :] = (
                    glu[s * Lc:(s + 1) * Lc, b * LANES:(b + 1) * LANES])
        return convbuf[ns - 1, (i + 1) * (quarter // LANES) - 1,
                       HIST_ROWS + Lc - SUBLANES:HIST_ROWS + Lc, :]

    rb = min(CONV_ROW_BLOCK, Lc)

    def zero_tile(tile):
        bits = pltpu.bitcast(tile, jnp.uint32)
        zero = lax.shift_right_logical(lax.shift_right_logical(bits, jnp.uint32(16)), jnp.uint32(16))
        return pltpu.bitcast(zero, F32)

    def zero_after(tile):
        return zero_tile(tile)[0:1]

    def conv_block(i, after=None):
        cs = slice(i * LANES, (i + 1) * LANES)
        bias = bdw[:, cs] if after is None else bdw[:, cs] + zero_after(after)
        for s in range(ns):
            for r0 in range(0, Lc, rb):
                acc = jnp.broadcast_to(bias, (rb, LANES))
                for t in range(CONV_WIDTH):
                    wt = jnp.broadcast_to(wdw[t:t + 1, cs], (SUBLANES, LANES))
                    if t > 0:
                        wt = wt + zero_tile(acc[0:SUBLANES])
                    acc = acc + (convbuf[s, i, lo + r0 + t:lo + r0 + t + rb, :]
                                 * pltpu.repeat(wt, rb // SUBLANES, axis=0))
                dw_scr[s * Lc + r0:s * Lc + r0 + rb, cs] = acc

    def conv_tail():
        for s in range(ns):
            for b in range(n_blk):
                tail = convbuf[s, b, HIST_ROWS + Lc - CONV_BUF:HIST_ROWS + Lc, :]
                conv_out[s, :, b * LANES:(b + 1) * LANES] = tail
                if carried:
                    convbuf[s, b, lo:HIST_ROWS, :] = tail

    def ln_swish(r0, r1, after):
        y = _layer_norm(dw_scr[r0:r1, :], lnag[...], lnab[...] + zero_after(after)[:, 0:1])
        yain_scr[r0:r1, :] = (y * _sigmoid(y)).astype(BF16)

    def gates():
        xb = xb_scr[...]
        zgc = _mm(xb, _w(wgc)) + bgc[...]
        ig_col = zgc[:, :LANES]
        lf_col = _log_sigmoid(zgc[:, LANES:])
        ig_row = _mm_nt(wgri[...], xb) + bgr[:, 0:1]
        lf_row = _log_sigmoid(_mm_nt(wgrf[...], xb) + bgr[:, 1:2])
        rows = lax.broadcasted_iota(jnp.int32, (L, L), 0)
        cols = lax.broadcasted_iota(jnp.int32, (L, L), 1)
        causal = rows >= cols
        tri_l = causal.astype(F32)
        tri_u = (rows <= cols).astype(F32)
        per_chunk = []
        for ci in range(ns * nc):
            sl = slice(ci * L, (ci + 1) * L)
            b_col = _mm_f32(tri_l, lf_col[sl])
            b_row = _mm_f32(lf_row[:, sl], tri_u)
            per_chunk.append((ig_col[sl], ig_row[:, sl], b_col, b_row))
        return causal, per_chunk

    def qkvo_head(h):
        hs = slice(h * D_MODEL, (h + 1) * D_MODEL)
        zq_scr[:, hs] = _mm(xb_scr[...], _w(wqkvo, hs))
        return zq_scr[T - SUBLANES:T, (h + 1) * D_MODEL - LANES:(h + 1) * D_MODEL]

    lane = lax.broadcasted_iota(jnp.int32, (1, LANES), 1)
    scale = HEAD_DIM ** -0.5

    def mlstm_head(s, c, h, causal, chunk_gates, after):
        ig_col, ig_row, b_col, b_row = chunk_gates
        sl = slice((s * nc + c) * L, (s * nc + c + 1) * L)
        first = c == 0
        c_prev, n_prev, m_prev = ((c_out, n_out, m_out) if carried or not first
                                  else (c_in, n_in, m_in))
        b_c = b_col[:, h:h + 1] + zero_after(after)[:, 0:1]
        ig_c = ig_col[:, h:h + 1]
        b_r = b_row[h:h + 1, :]
        ig_r = ig_row[h:h + 1, :]
        m0 = m_prev[s][:, h:h + 1]
        a_c = b_c + m0
        dm = jnp.where(causal, b_c + (ig_r - b_r), MASKED)
        m_c = jnp.maximum(a_c, jnp.max(dm, axis=1, keepdims=True))
        w_intra = jnp.exp(dm - m_c)
        w_inter = jnp.exp(a_c - m_c)

        base = h * D_MODEL
        q = zq_scr[sl, base:base + HEAD_DIM]
        k = zq_scr[sl, base + HEAD_DIM:base + 2 * HEAD_DIM] * scale
        v = zq_scr[sl, base + 2 * HEAD_DIM:base + 3 * HEAD_DIM]
        o = zq_scr[sl, base + 3 * HEAD_DIM:base + 4 * HEAD_DIM]
        qb = q.astype(BF16)
        vb = v.astype(BF16)
        c0 = c_prev[s, h]
        n0 = n_prev[s, h]

        smat = _mm_nt(qb, k.astype(BF16)) * w_intra
        num = _mm(smat.astype(BF16), vb) + w_inter * _mm(qb, c0.astype(BF16))
        den = (jnp.sum(smat, axis=1, keepdims=True)
               + w_inter * jnp.sum(q * n0, axis=1, keepdims=True))
        hh = num / jnp.maximum(jnp.abs(den), jnp.exp(-m_c))

        m_end = m_c[L - 1:L]
        w_end = jnp.exp(b_c[L - 1:L] - b_c + ig_c - m_end)
        decay = jnp.exp(a_c[L - 1:L] - m_end)
        kw = k * w_end
        c_out[s, h] = decay * c0 + _mm_tn(kw.astype(BF16), vb)
        n_out[s, h] = decay * n0 + jnp.sum(kw, axis=0, keepdims=True)
        m_out[s] = jnp.where(lane == h, m_end, m_out[s] if (carried or not first or h > 0)
                             else m_in[s])

        mu = jnp.mean(hh, axis=1, keepdims=True)
        hc = hh - mu
        var = jnp.mean(hc * hc, axis=1, keepdims=True)
        hn = hc * lax.rsqrt(var + LN_EPS) * hng[:, h * HEAD_DIM:(h + 1) * HEAD_DIM]
        ybuf[sl, h * HEAD_DIM:(h + 1) * HEAD_DIM] = (_sigmoid(o) * hn).astype(BF16)

    def merge_gates_quarter(i):
        cs = slice(i * quarter, (i + 1) * quarter)
        gs = slice(D_MODEL + i * quarter, D_MODEL + (i + 1) * quarter)
        xb = xb_scr[...]
        zg_scr[:, cs] = _sigmoid(_mm(xb, _w(wg, cs)))
        zg_scr[:, gs] = _sigmoid(_mm(xb, _w(wg, gs)))
        return zg_scr[T - SUBLANES:T, D_MODEL + (i + 1) * quarter - LANES:D_MODEL + (i + 1) * quarter]

    causal, chunk_gates = gates()
    a0 = glu_quarter(0)
    conv_block(0)
    a1 = glu_quarter(1)
    conv_block(1, after=a0)
    a2 = glu_quarter(2)
    conv_block(2, after=a1)
    a3 = glu_quarter(3)
    conv_block(3, after=a2)
    b0 = qkvo_head(0)
    conv_block(4, after=a3)
    conv_block(5, after=a3)
    b1 = qkvo_head(1)
    conv_block(6, after=b0)
    conv_block(7, after=b0)
    conv_tail()
    b2 = qkvo_head(2)

    def heads(h, after):
        for s in range(ns):
            for c in range(nc):
                mlstm_head(s, c, h, causal, chunk_gates[s * nc + c], after)

    half = T // 2
    heads(0, b1)
    b3 = qkvo_head(3)
    heads(1, b2)
    merge_gates_quarter(0)
    c1 = merge_gates_quarter(1)
    ln_swish(0, half, b3)
    heads(2, b3)
    merge_gates_quarter(2)
    merge_gates_quarter(3)
    ln_swish(half, T, c1)
    heads(3, c1)

    y_a = _mm(yain_scr[...], _w(waout))
    y_b = _mm(ybuf[...], _w(wbout))
    mrg_scr[...] = (zg_scr[:, :D_MODEL] * y_a + zg_scr[:, D_MODEL:] * y_b).astype(BF16)
    res = ALPHA * x_ref[...] + _mm(mrg_scr[...], _w(wout))
    x1_ref[...] = _layer_norm(res, ln1g[...], ln1b[...])


def _ffn_kernel(x1_ref, wff1, wff2, ln2g, ln2b, out_ref):
    x1 = x1_ref[...]
    hid = jnp.maximum(_mm(x1.astype(BF16), _w(wff1)), 0.0)
    ff = _mm((hid * hid).astype(BF16), _w(wff2))
    out_ref[...] = _layer_norm(ALPHA * x1 + ff, ln2g[...], ln2b[...])


def _resident(shape):
    nd = len(shape)
    return pl.BlockSpec(shape, lambda *_: (0,) * nd, pipeline_mode=pl.Buffered(1))


def _mixer(x2d, cache, c0, n0, m0, weights, *, ns, nc, L, nj):
    Lc = nc * L
    T = ns * Lc
    n_tok = x2d.shape[0]
    G = n_tok // (T * nj)
    n_seq = G * ns
    state_specs = [
        pl.BlockSpec((ns, CONV_BUF, D_MODEL), lambda g, j: (g, 0, 0)),
        pl.BlockSpec((ns, N_HEADS, HEAD_DIM, HEAD_DIM), lambda g, j: (g, 0, 0, 0)),
        pl.BlockSpec((ns, N_HEADS, 1, HEAD_DIM), lambda g, j: (g, 0, 0, 0)),
        pl.BlockSpec((ns, 1, LANES), lambda g, j: (g, 0, 0)),
    ]
    if ns > 1:
        state_specs[1] = pl.BlockSpec((ns, N_HEADS, HEAD_DIM, HEAD_DIM), lambda g, j: (g, 0, 0, 0),
                                      pipeline_mode=pl.Buffered(1))
    state_in_specs = state_specs
    x_spec = pl.BlockSpec((T, D_MODEL), lambda g, j: (g * nj + j, 0))
    return pl.pallas_call(
        functools.partial(_mixer_kernel, ns, nc, L, nj),
        grid=(G, nj),
        in_specs=[x_spec] + state_in_specs + [_resident(w.shape) for w in weights],
        out_specs=[x_spec] + state_specs,
        out_shape=[
            jax.ShapeDtypeStruct((n_tok, D_MODEL), F32),
            jax.ShapeDtypeStruct((n_seq, CONV_BUF, D_MODEL), F32),
            jax.ShapeDtypeStruct((n_seq, N_HEADS, HEAD_DIM, HEAD_DIM), F32),
            jax.ShapeDtypeStruct((n_seq, N_HEADS, 1, HEAD_DIM), F32),
            jax.ShapeDtypeStruct((n_seq, 1, LANES), F32),
        ],
        scratch_shapes=[
            pltpu.VMEM((T, D_MODEL), BF16),
            pltpu.VMEM((ns, D_MODEL // LANES, HIST_ROWS + Lc, LANES), F32),
            pltpu.VMEM((T, D_MODEL), F32),
            pltpu.VMEM((T, D_MODEL), BF16),
            pltpu.VMEM((T, 4 * D_MODEL), F32),
            pltpu.VMEM((T, 2 * D_MODEL), F32),
            pltpu.VMEM((T, D_MODEL), BF16),
            pltpu.VMEM((T, D_MODEL), BF16),
        ],
        compiler_params=pltpu.CompilerParams(
            dimension_semantics=("arbitrary", "arbitrary"),
            vmem_limit_bytes=VMEM_LIMIT_BYTES),
        name=f"mixer_ns{ns}",
    )(x2d, cache, c0, n0, m0, *weights)


def _ffn(x1, wff1, wff2, ln2g, ln2b, *, tile):
    n_tok = x1.shape[0]
    x_spec = pl.BlockSpec((tile, D_MODEL), lambda i: (i, 0))
    weights = (wff1, wff2, ln2g, ln2b)
    return pl.pallas_call(
        _ffn_kernel,
        grid=(n_tok // tile,),
        in_specs=[x_spec] + [_resident(w.shape) for w in weights],
        out_specs=x_spec,
        out_shape=jax.ShapeDtypeStruct((n_tok, D_MODEL), F32),
        compiler_params=pltpu.CompilerParams(
            dimension_semantics=("arbitrary",),
            vmem_limit_bytes=VMEM_LIMIT_BYTES),
        name="ffn",
    )(x1, *weights)


def _pad_m(m):
    return jnp.pad(m.astype(F32), ((0, 0), (0, LANES - N_HEADS)))[:, None, :]


def kernel(x_prompt, x_sample, cache_conv, state_C, state_n, state_m, w_in, b_gate, w_dw, b_dw,
           ln_a_g, ln_a_b, w_a_out, hn_g, w_b_out, w_out, ln1_g, ln1_b, w_ff1, w_ff2, ln2_g, ln2_b):
    assert w_in.shape[0] == DEPTH == 1
    bp, sp, _ = x_prompt.shape
    bs, ss, _ = x_sample.shape
    e = D_MODEL
    wi = w_in[0]
    o_qkvo = 2 * e
    o_gate = o_qkvo + 4 * e
    o_ga = o_gate + 2 * N_HEADS

    def row(v):
        return v[0].astype(F32)[None, :]

    w_qkvo = (wi[:, o_qkvo:o_gate].reshape(e, 4, N_HEADS, HEAD_DIM)
              .transpose(0, 2, 1, 3).reshape(e, 4 * e))
    w_i = wi[:, o_gate:o_gate + N_HEADS]
    w_f = wi[:, o_gate + N_HEADS:o_ga]
    zpad = jnp.zeros((e, LANES - N_HEADS), wi.dtype)
    wgc = jnp.concatenate([w_i, zpad, w_f, zpad], axis=1)
    rpad = jnp.zeros((SUBLANES - N_HEADS, e), wi.dtype)
    wgri = jnp.concatenate([w_i.T, rpad], axis=0).astype(BF16)
    wgrf = jnp.concatenate([w_f.T, rpad], axis=0).astype(BF16)
    bg = b_gate[0].astype(F32)
    bpad = jnp.zeros((LANES - N_HEADS,), F32)
    bgc = jnp.concatenate([bg[:N_HEADS], bpad, bg[N_HEADS:], bpad])[None, :]
    bgr = jnp.pad(bg.reshape(2, N_HEADS).T, ((0, SUBLANES - N_HEADS), (0, 0)))

    weights = (
        _pack_rows(wi[:, :o_qkvo]), _pack_rows(w_qkvo), _pack_rows(wi[:, o_ga:]),
        _pack_rows(wgc), wgri, wgrf, bgc, bgr,
        w_dw[0].astype(F32), row(b_dw), row(ln_a_g), row(ln_a_b),
        _pack_rows(w_a_out[0]), row(hn_g), _pack_rows(w_b_out[0]), _pack_rows(w_out[0]),
        row(ln1_g), row(ln1_b),
    )
    ffn_w = (_pack_rows(w_ff1[0]), _pack_rows(w_ff2[0]), row(ln2_g), row(ln2_b))

    Lp = MXU_DIM
    ncp = 1
    x1p, convp, cp, np_, mp = _mixer(
        x_prompt.reshape(bp * sp, e),
        jnp.zeros((bp, CONV_BUF, e), F32),
        jnp.zeros((bp, N_HEADS, HEAD_DIM, HEAD_DIM), F32),
        jnp.zeros((bp, N_HEADS, 1, HEAD_DIM), F32),
        jnp.zeros((bp, 1, LANES), F32),
        weights, ns=1, nc=ncp, L=Lp, nj=sp // (ncp * Lp))
    x1s, convs, cs, ns_, ms = _mixer(
        x_sample.reshape(bs * ss, e),
        cache_conv[0].astype(F32),
        state_C[0].astype(F32),
        state_n[0].astype(F32)[:, :, None, :],
        _pad_m(state_m[0]),
        weights, ns=4, nc=1, L=ss, nj=1)

    yp = _ffn(x1p, *ffn_w, tile=512).reshape(bp, sp, e)
    ys = _ffn(x1s, *ffn_w, tile=512).reshape(bs, ss, e)

    def st(conv, c, n, m):
        return conv[None], c[None], n[:, :, 0, :][None], m[:, 0, :N_HEADS][None]

    return (yp, ys) + st(convp, cp, np_, mp) + st(convs, cs, ns_, ms)
```

```python
import functools

import jax
import jax.numpy as jnp
from jax import lax
from jax.experimental import pallas as pl
from jax.experimental.pallas import tpu as pltpu

D_MODEL = 1024
N_HEADS = 4
HEAD_DIM = 256
CONV_WIDTH = 31
CONV_BUF = CONV_WIDTH - 1
D_FF = 4 * D_MODEL
DEPTH = 1
ALPHA = (2.0 * DEPTH) ** 0.25
LN_EPS = 1e-5

LANES = 128
SUBLANES = 8
MXU_DIM = 256
HIST_ROWS = 32
CONV_ROW_BLOCK = 256
MASKED = -1e30
VMEM_LIMIT_BYTES = 58 * 1024 * 1024

F32 = jnp.float32
BF16 = jnp.bfloat16


def _mm(a, b):
    return jnp.dot(a, b, preferred_element_type=F32)


def _mm_nt(a, b):
    return lax.dot_general(a, b, (((1,), (1,)), ((), ())), preferred_element_type=F32)


def _mm_tn(a, b):
    return lax.dot_general(a, b, (((0,), (0,)), ((), ())), preferred_element_type=F32)


def _bf16_terms(x):
    hi = x.astype(BF16)
    r1 = x - hi.astype(F32)
    mid = r1.astype(BF16)
    lo = (r1 - mid.astype(F32)).astype(BF16)
    return hi, mid, lo


def _ones_mm(ones, x):
    ob = ones.astype(BF16)
    hi, mid, lo = _bf16_terms(x)
    return _mm(ob, hi) + _mm(ob, mid) + _mm(ob, lo)


def _mm_ones(x, ones):
    ob = ones.astype(BF16)
    hi, mid, lo = _bf16_terms(x)
    return _mm(hi, ob) + _mm(mid, ob) + _mm(lo, ob)


PACK_ROWS = 1024
PACK_COLS = 512


def _pack_kernel(w_ref, o_ref):
    o_ref[...] = pltpu.bitcast(w_ref[...].astype(BF16), jnp.uint32)


def _pack_rows(w, cols=None, col_block=PACK_COLS, col_map=lambda j: j):
    if w.ndim == 2:
        w = w[None]
    _, k, n = w.shape
    cols = n if cols is None else cols
    kb = min(PACK_ROWS, k)
    return pl.pallas_call(
        _pack_kernel,
        grid=(k // kb, cols // col_block),
        in_specs=[pl.BlockSpec((None, kb, col_block), lambda i, j: (0, i, col_map(j)))],
        out_specs=pl.BlockSpec((kb // 2, col_block), lambda i, j: (i, j)),
        out_shape=jax.ShapeDtypeStruct((k // 2, cols), jnp.uint32),
        compiler_params=pltpu.CompilerParams(dimension_semantics=("arbitrary", "arbitrary")),
        name="pack_weight",
    )(w)


def _w(ref, cols=slice(None)):
    return pltpu.bitcast(ref[:, cols], BF16)


def _sigmoid(x):
    return 0.5 * (jnp.tanh(0.5 * x) + 1.0)


def _log_sigmoid(x):
    return jnp.minimum(x, 0.0) - jnp.log(1.0 + jnp.exp(-jnp.abs(x)))


def _layer_norm(x, g, b):
    mu = jnp.mean(x, axis=-1, keepdims=True)
    xc = x - mu
    var = jnp.mean(xc * xc, axis=-1, keepdims=True)
    return xc * lax.rsqrt(var + LN_EPS) * g + b


def _mixer_kernel(ns, nc, L, nj,
                  x_ref, cache_ref, c_in, n_in, m_in,
                  wconv, wqkvo, wg, wgc, wgri, wgrf, bgc, bgr, wdw, bdw, lnag, lnab,
                  waout, hng, wbout, wout, ln1g, ln1b,
                  x1_ref, conv_out, c_out, n_out, m_out,
                  xb_scr, cb0, cb1, cb2, cb3, dw_scr, yain_scr, zq_scr, zg_scr, ybuf, mrg_scr):
    j = pl.program_id(1)
    carried = nj > 1
    Lc = nc * L
    T = ns * Lc
    lo = HIST_ROWS - CONV_BUF
    quarter = D_MODEL // 4

    n_blk = D_MODEL // LANES
    per_q = quarter // LANES
    windows = (cb0, cb1, cb2, cb3)

    def window(b):
        return windows[b // per_q]

    def load_history():
        for s in range(ns):
            for b in range(n_blk):
                window(b)[s, b % per_q, lo:HIST_ROWS, :] = cache_ref[s, :, b * LANES:(b + 1) * LANES]

    if carried:
        @pl.when(j == 0)
        def _():
            c_out[...] = c_in[...]
            n_out[...] = n_in[...]
            m_out[...] = m_in[...]
            load_history()
    else:
        load_history()

    def zero_tile(tile):
        bits = pltpu.bitcast(tile, jnp.uint32)
        zero = lax.shift_right_logical(lax.shift_right_logical(bits, jnp.uint32(16)), jnp.uint32(16))
        return pltpu.bitcast(zero, F32)

    def zero_after(tile):
        return zero_tile(tile)[0:1]

    xb_scr[...] = x_ref[...].astype(BF16)

    def glu_quarter(i):
        cs = slice(i * quarter, (i + 1) * quarter)
        gs = slice(D_MODEL + i * quarter, D_MODEL + (i + 1) * quarter)
        xb = xb_scr[...]
        glu = _mm(xb, _w(wconv, cs)) * _sigmoid(_mm(xb, _w(wconv, gs)))
        for s in range(ns):
            for b in range(quarter // LANES):
                windows[i][s, b, HIST_ROWS:HIST_ROWS + Lc, :] = (
                    glu[s * Lc:(s + 1) * Lc, b * LANES:(b + 1) * LANES])
        return windows[i][ns - 1, per_q - 1,
                       HIST_ROWS + Lc - SUBLANES:HIST_ROWS + Lc, :]

    rb = min(CONV_ROW_BLOCK, Lc)

    def conv_block(i, after=None):
        cs = slice(i * LANES, (i + 1) * LANES)
        bias = bdw[:, cs] if after is None else bdw[:, cs] + zero_after(after)
        for s in range(ns):
            for r0 in range(0, Lc, rb):
                acc = jnp.broadcast_to(bias, (rb, LANES))
                for t in range(CONV_WIDTH):
                    wt = jnp.broadcast_to(wdw[t:t + 1, cs], (SUBLANES, LANES))
                    if t > 0:
                        wt = wt + zero_tile(acc[0:SUBLANES])
                    win = window(i)[s, i % per_q, lo + r0 + t:lo + r0 + t + rb, :]
                    acc = acc + (win.reshape(rb // SUBLANES, SUBLANES, LANES) * wt).reshape(rb, LANES)
                dw_scr[s * Lc + r0:s * Lc + r0 + rb, cs] = acc

    def conv_tail():
        for s in range(ns):
            for b in range(n_blk):
                tail = window(b)[s, b % per_q, HIST_ROWS + Lc - CONV_BUF:HIST_ROWS + Lc, :]
                conv_out[s, :, b * LANES:(b + 1) * LANES] = tail
                if carried:
                    window(b)[s, b % per_q, lo:HIST_ROWS, :] = tail

    def ln_swish(r0, r1, after):
        y = _layer_norm(dw_scr[r0:r1, :], lnag[...], lnab[...] + zero_after(after)[:, 0:1])
        yain_scr[r0:r1, :] = (y * _sigmoid(y)).astype(BF16)

    def gates():
        xb = xb_scr[...]
        zgc = _mm(xb, _w(wgc)) + bgc[...]
        ig_col = zgc[:, :LANES]
        lf_col = _log_sigmoid(zgc[:, LANES:])
        ig_row = _mm_nt(wgri[...], xb) + bgr[:, 0:1]
        lf_row = _log_sigmoid(_mm_nt(wgrf[...], xb) + bgr[:, 1:2])
        rows = lax.broadcasted_iota(jnp.int32, (L, L), 0)
        cols = lax.broadcasted_iota(jnp.int32, (L, L), 1)
        causal = rows >= cols
        tri_l = causal.astype(F32)
        tri_u = (rows <= cols).astype(F32)
        per_chunk = []
        for ci in range(ns * nc):
            sl = slice(ci * L, (ci + 1) * L)
            b_col = _ones_mm(tri_l, lf_col[sl])
            b_row = _mm_ones(lf_row[:, sl], tri_u)
            per_chunk.append((ig_col[sl], ig_row[:, sl], b_col, b_row))
        return causal, per_chunk

    def qkvo_head(h):
        hs = slice(h * D_MODEL, (h + 1) * D_MODEL)
        zq_scr[:, hs] = _mm(xb_scr[...], _w(wqkvo, hs))
        return zq_scr[T - SUBLANES:T, (h + 1) * D_MODEL - LANES:(h + 1) * D_MODEL]

    lane = lax.broadcasted_iota(jnp.int32, (1, LANES), 1)
    scale = HEAD_DIM ** -0.5

    def mlstm_head(s, c, h, causal, chunk_gates, after):
        ig_col, ig_row, b_col, b_row = chunk_gates
        sl = slice((s * nc + c) * L, (s * nc + c + 1) * L)
        first = c == 0
        c_prev, n_prev, m_prev = ((c_out, n_out, m_out) if carried or not first
                                  else (c_in, n_in, m_in))
        b_c = b_col[:, h:h + 1] + zero_after(after)[:, 0:1]
        ig_c = ig_col[:, h:h + 1]
        b_r = b_row[h:h + 1, :]
        ig_r = ig_row[h:h + 1, :]
        m0 = m_prev[s][:, h:h + 1]
        a_c = b_c + m0
        dm = jnp.where(causal, b_c + (ig_r - b_r), MASKED)
        m_c = jnp.maximum(a_c, jnp.max(dm, axis=1, keepdims=True))
        w_intra = jnp.exp(dm - m_c)
        w_inter = jnp.exp(a_c - m_c)

        base = h * D_MODEL
        q = zq_scr[sl, base:base + HEAD_DIM]
        k = zq_scr[sl, base + HEAD_DIM:base + 2 * HEAD_DIM] * scale
        v = zq_scr[sl, base + 2 * HEAD_DIM:base + 3 * HEAD_DIM]
        o = zq_scr[sl, base + 3 * HEAD_DIM:base + 4 * HEAD_DIM]
        qb = q.astype(BF16)
        vb = v.astype(BF16)
        c0 = c_prev[s, h]
        n0 = n_prev[s, h]

        smat = _mm_nt(qb, k.astype(BF16)) * w_intra
        num = _mm(smat.astype(BF16), vb) + w_inter * _mm(qb, c0.astype(BF16))
        den = (jnp.sum(smat, axis=1, keepdims=True)
               + w_inter * jnp.sum(q * n0, axis=1, keepdims=True))
        hh = num / jnp.maximum(jnp.abs(den), jnp.exp(-m_c))

        m_end = m_c[L - 1:L]
        w_end = jnp.exp(b_c[L - 1:L] - b_c + ig_c - m_end)
        decay = jnp.exp(a_c[L - 1:L] - m_end)
        kw = k * w_end
        c_out[s, h] = decay * c0 + _mm_tn(kw.astype(BF16), vb)
        n_out[s, h] = decay * n0 + jnp.sum(kw, axis=0, keepdims=True)
        m_out[s] = jnp.where(lane == h, m_end, m_out[s] if (carried or not first or h > 0)
                             else m_in[s])

        mu = jnp.mean(hh, axis=1, keepdims=True)
        hc = hh - mu
        var = jnp.mean(hc * hc, axis=1, keepdims=True)
        hn = hc * lax.rsqrt(var + LN_EPS) * hng[:, h * HEAD_DIM:(h + 1) * HEAD_DIM]
        ybuf[sl, h * HEAD_DIM:(h + 1) * HEAD_DIM] = (_sigmoid(o) * hn).astype(BF16)

    def merge_gates_quarter(i):
        cs = slice(i * quarter, (i + 1) * quarter)
        gs = slice(D_MODEL + i * quarter, D_MODEL + (i + 1) * quarter)
        xb = xb_scr[...]
        zg_scr[:, cs] = _sigmoid(_mm(xb, _w(wg, cs)))
        zg_scr[:, gs] = _sigmoid(_mm(xb, _w(wg, gs)))
        return zg_scr[T - SUBLANES:T, D_MODEL + (i + 1) * quarter - LANES:D_MODEL + (i + 1) * quarter]

    causal, chunk_gates = gates()
    a0 = glu_quarter(0)
    conv_block(0)
    a1 = glu_quarter(1)
    conv_block(1, after=a0)
    a2 = glu_quarter(2)
    conv_block(2, after=a1)
    a3 = glu_quarter(3)
    conv_block(3, after=a2)
    b0 = qkvo_head(0)
    conv_block(4, after=a3)
    conv_block(5, after=a3)
    b1 = qkvo_head(1)
    conv_block(6, after=b0)
    conv_block(7, after=b0)
    conv_tail()
    b2 = qkvo_head(2)

    def heads(h, after):
        for s in range(ns):
            for c in range(nc):
                mlstm_head(s, c, h, causal, chunk_gates[s * nc + c], after)

    half = T // 2
    heads(0, b1)
    b3 = qkvo_head(3)
    heads(1, b2)
    merge_gates_quarter(0)
    c1 = merge_gates_quarter(1)
    ln_swish(0, half, b3)
    heads(2, b3)
    merge_gates_quarter(2)
    merge_gates_quarter(3)
    ln_swish(half, T, c1)
    heads(3, c1)

    y_a = _mm(yain_scr[...], _w(waout))
    y_b = _mm(ybuf[...], _w(wbout))
    mrg_scr[...] = (zg_scr[:, :D_MODEL] * y_a + zg_scr[:, D_MODEL:] * y_b).astype(BF16)
    res = ALPHA * x_ref[...] + _mm(mrg_scr[...], _w(wout))
    x1_ref[...] = _layer_norm(res, ln1g[...], ln1b[...])


def _ffn_kernel(x1_ref, wff1, wff2, ln2g, ln2b, out_ref):
    x1 = x1_ref[...]
    hid = jnp.maximum(_mm(x1.astype(BF16), _w(wff1)), 0.0)
    ff = _mm((hid * hid).astype(BF16), _w(wff2))
    out_ref[...] = _layer_norm(ALPHA * x1 + ff, ln2g[...], ln2b[...])


def _resident(shape):
    nd = len(shape)
    return pl.BlockSpec(shape, lambda *_: (0,) * nd, pipeline_mode=pl.Buffered(1))


def _mixer(x2d, cache, c0, n0, m0, weights, *, ns, nc, L, nj):
    Lc = nc * L
    T = ns * Lc
    n_tok = x2d.shape[0]
    G = n_tok // (T * nj)
    n_seq = G * ns
    state_specs = [
        pl.BlockSpec((ns, CONV_BUF, D_MODEL), lambda g, j: (g, 0, 0)),
        pl.BlockSpec((ns, N_HEADS, HEAD_DIM, HEAD_DIM), lambda g, j: (g, 0, 0, 0)),
        pl.BlockSpec((ns, N_HEADS, 1, HEAD_DIM), lambda g, j: (g, 0, 0, 0)),
        pl.BlockSpec((ns, 1, LANES), lambda g, j: (g, 0, 0)),
    ]
    if ns > 1:
        state_specs[1] = pl.BlockSpec((ns, N_HEADS, HEAD_DIM, HEAD_DIM), lambda g, j: (g, 0, 0, 0),
                                      pipeline_mode=pl.Buffered(1))
    state_in_specs = state_specs
    x_spec = pl.BlockSpec((T, D_MODEL), lambda g, j: (g * nj + j, 0))
    return pl.pallas_call(
        functools.partial(_mixer_kernel, ns, nc, L, nj),
        grid=(G, nj),
        in_specs=[x_spec] + state_in_specs + [_resident(w.shape) for w in weights],
        out_specs=[x_spec] + state_specs,
        out_shape=[
            jax.ShapeDtypeStruct((n_tok, D_MODEL), F32),
            jax.ShapeDtypeStruct((n_seq, CONV_BUF, D_MODEL), F32),
            jax.ShapeDtypeStruct((n_seq, N_HEADS, HEAD_DIM, HEAD_DIM), F32),
            jax.ShapeDtypeStruct((n_seq, N_HEADS, 1, HEAD_DIM), F32),
            jax.ShapeDtypeStruct((n_seq, 1, LANES), F32),
        ],
        scratch_shapes=[
            pltpu.VMEM((T, D_MODEL), BF16),
            *[pltpu.VMEM((ns, D_MODEL // 4 // LANES, HIST_ROWS + Lc, LANES), F32)
              for _ in range(4)],
            pltpu.VMEM((T, D_MODEL), F32),
            pltpu.VMEM((T, D_MODEL), BF16),
            pltpu.VMEM((T, 4 * D_MODEL), F32),
            pltpu.VMEM((T, 2 * D_MODEL), F32),
            pltpu.VMEM((T, D_MODEL), BF16),
            pltpu.VMEM((T, D_MODEL), BF16),
        ],
        compiler_params=pltpu.CompilerParams(
            dimension_semantics=("arbitrary", "arbitrary"),
            vmem_limit_bytes=VMEM_LIMIT_BYTES),
        name=f"mixer_ns{ns}",
    )(x2d, cache, c0, n0, m0, *weights)


def _ffn(x1, wff1, wff2, ln2g, ln2b, *, tile):
    n_tok = x1.shape[0]
    x_spec = pl.BlockSpec((tile, D_MODEL), lambda i: (i, 0))
    weights = (wff1, wff2, ln2g, ln2b)
    return pl.pallas_call(
        _ffn_kernel,
        grid=(n_tok // tile,),
        in_specs=[x_spec] + [_resident(w.shape) for w in weights],
        out_specs=x_spec,
        out_shape=jax.ShapeDtypeStruct((n_tok, D_MODEL), F32),
        compiler_params=pltpu.CompilerParams(
            dimension_semantics=("arbitrary",),
            vmem_limit_bytes=VMEM_LIMIT_BYTES),
        name="ffn",
    )(x1, *weights)


def _pad_m(m):
    return jnp.pad(m.astype(F32), ((0, 0), (0, LANES - N_HEADS)))[:, None, :]


def kernel(x_prompt, x_sample, cache_conv, state_C, state_n, state_m, w_in, b_gate, w_dw, b_dw,
           ln_a_g, ln_a_b, w_a_out, hn_g, w_b_out, w_out, ln1_g, ln1_b, w_ff1, w_ff2, ln2_g, ln2_b):
    assert w_in.shape[0] == DEPTH == 1
    bp, sp, _ = x_prompt.shape
    bs, ss, _ = x_sample.shape
    e = D_MODEL
    wi = w_in[0]
    o_qkvo = 2 * e
    o_gate = o_qkvo + 4 * e
    o_ga = o_gate + 2 * N_HEADS

    def row(v):
        return v[0].astype(F32)[None, :]

    qkvo_blk0 = o_qkvo // HEAD_DIM

    def qkvo_block(j):
        return qkvo_blk0 + (j % 4) * N_HEADS + j // 4

    w_i = wi[:, o_gate:o_gate + N_HEADS]
    w_f = wi[:, o_gate + N_HEADS:o_ga]
    zpad = jnp.zeros((e, LANES - N_HEADS), wi.dtype)
    wgc = jnp.concatenate([w_i, zpad, w_f, zpad], axis=1)
    rpad = jnp.zeros((SUBLANES - N_HEADS, e), wi.dtype)
    wgri = jnp.concatenate([w_i.T, rpad], axis=0).astype(BF16)
    wgrf = jnp.concatenate([w_f.T, rpad], axis=0).astype(BF16)
    bg = b_gate[0].astype(F32)
    bpad = jnp.zeros((LANES - N_HEADS,), F32)
    bgc = jnp.concatenate([bg[:N_HEADS], bpad, bg[N_HEADS:], bpad])[None, :]
    bgr = jnp.pad(bg.reshape(2, N_HEADS).T, ((0, SUBLANES - N_HEADS), (0, 0)))

    weights = (
        _pack_rows(w_in, cols=o_qkvo),
        _pack_rows(w_in, cols=4 * e, col_block=HEAD_DIM, col_map=qkvo_block),
        _pack_rows(wi[:, o_ga:]),
        _pack_rows(wgc, col_block=2 * LANES), wgri, wgrf, bgc, bgr,
        w_dw[0].astype(F32), row(b_dw), row(ln_a_g), row(ln_a_b),
        _pack_rows(w_a_out), row(hn_g), _pack_rows(w_b_out), _pack_rows(w_out),
        row(ln1_g), row(ln1_b),
    )
    ffn_w = (_pack_rows(w_ff1), _pack_rows(w_ff2), row(ln2_g), row(ln2_b))

    Lp = MXU_DIM
    ncp = 1
    x1p, convp, cp, np_, mp = _mixer(
        x_prompt.reshape(bp * sp, e),
        jnp.zeros((bp, CONV_BUF, e), F32),
        jnp.zeros((bp, N_HEADS, HEAD_DIM, HEAD_DIM), F32),
        jnp.zeros((bp, N_HEADS, 1, HEAD_DIM), F32),
        jnp.zeros((bp, 1, LANES), F32),
        weights, ns=1, nc=ncp, L=Lp, nj=sp // (ncp * Lp))
    x1s, convs, cs, ns_, ms = _mixer(
        x_sample.reshape(bs * ss, e),
        cache_conv[0].astype(F32),
        state_C[0].astype(F32),
        state_n[0].astype(F32)[:, :, None, :],
        _pad_m(state_m[0]),
        weights, ns=4, nc=1, L=ss, nj=1)

    yp = _ffn(x1p, *ffn_w, tile=512).reshape(bp, sp, e)
    ys = _ffn(x1s, *ffn_w, tile=512).reshape(bs, ss, e)

    def st(conv, c, n, m):
        return conv[None], c[None], n[:, :, 0, :][None], m[:, 0, :N_HEADS][None]

    return (yp, ys) + st(convp, cp, np_, mp) + st(convs, cs, ns_, ms)
```

```python
import functools

import jax
import jax.numpy as jnp
from jax import lax
from jax.experimental import pallas as pl
from jax.experimental.pallas import tpu as pltpu

D_MODEL = 1024
N_HEADS = 4
HEAD_DIM = 256
CONV_WIDTH = 31
CONV_BUF = CONV_WIDTH - 1
D_FF = 4 * D_MODEL
DEPTH = 1
ALPHA = (2.0 * DEPTH) ** 0.25
LN_EPS = 1e-5

LANES = 128
SUBLANES = 8
MXU_DIM = 256
HIST_ROWS = 32
CONV_ROW_BLOCK = 256
MASKED = -1e30
VMEM_LIMIT_BYTES = 58 * 1024 * 1024

F32 = jnp.float32
BF16 = jnp.bfloat16


def _mm(a, b):
    return jnp.dot(a, b, preferred_element_type=F32)


def _mm_nt(a, b):
    return lax.dot_general(a, b, (((1,), (1,)), ((), ())), preferred_element_type=F32)


def _mm_tn(a, b):
    return lax.dot_general(a, b, (((0,), (0,)), ((), ())), preferred_element_type=F32)


def _bf16_terms(x):
    hi = x.astype(BF16)
    r1 = x - hi.astype(F32)
    mid = r1.astype(BF16)
    lo = (r1 - mid.astype(F32)).astype(BF16)
    return hi, mid, lo


def _ones_mm(ones, x):
    ob = ones.astype(BF16)
    hi, mid, lo = _bf16_terms(x)
    return _mm(ob, hi) + _mm(ob, mid) + _mm(ob, lo)


def _mm_ones(x, ones):
    ob = ones.astype(BF16)
    hi, mid, lo = _bf16_terms(x)
    return _mm(hi, ob) + _mm(mid, ob) + _mm(lo, ob)


PACK_ROWS = 1024
PACK_COLS = 512


def _pack_kernel(w_ref, o_ref):
    o_ref[...] = pltpu.bitcast(w_ref[...].astype(BF16), jnp.uint32)


def _pack_rows(w, cols=None, col_block=PACK_COLS, col_map=lambda j: j):
    if w.ndim == 2:
        w = w[None]
    _, k, n = w.shape
    cols = n if cols is None else cols
    kb = min(PACK_ROWS, k)
    return pl.pallas_call(
        _pack_kernel,
        grid=(k // kb, cols // col_block),
        in_specs=[pl.BlockSpec((None, kb, col_block), lambda i, j: (0, i, col_map(j)))],
        out_specs=pl.BlockSpec((kb // 2, col_block), lambda i, j: (i, j)),
        out_shape=jax.ShapeDtypeStruct((k // 2, cols), jnp.uint32),
        compiler_params=pltpu.CompilerParams(dimension_semantics=("arbitrary", "arbitrary")),
        name="pack_weight",
    )(w)


def _pack_t_kernel(wt_ref, o_ref):
    o_ref[...] = pltpu.bitcast(wt_ref[...].T.astype(BF16), jnp.uint32)


def _pack_rows_t(wt, cols=None, col_block=PACK_COLS, col_map=lambda j: j):
    n, k = wt.shape
    cols = n if cols is None else cols
    return pl.pallas_call(
        _pack_t_kernel,
        grid=(cols // col_block,),
        in_specs=[pl.BlockSpec((col_block, k), lambda j: (col_map(j), 0))],
        out_specs=pl.BlockSpec((k // 2, col_block), lambda j: (0, j)),
        out_shape=jax.ShapeDtypeStruct((k // 2, cols), jnp.uint32),
        compiler_params=pltpu.CompilerParams(dimension_semantics=("arbitrary",)),
        name="pack_weight_t",
    )(wt)


def _w(ref, cols=slice(None)):
    return pltpu.bitcast(ref[:, cols], BF16)


def _sigmoid(x):
    return 0.5 * (jnp.tanh(0.5 * x) + 1.0)


def _log_sigmoid(x):
    return jnp.minimum(x, 0.0) - jnp.log(1.0 + jnp.exp(-jnp.abs(x)))


def _layer_norm(x, g, b):
    mu = jnp.mean(x, axis=-1, keepdims=True)
    xc = x - mu
    var = jnp.mean(xc * xc, axis=-1, keepdims=True)
    return xc * lax.rsqrt(var + LN_EPS) * g + b


def _mixer_kernel(ns, nc, L, nj,
                  x_ref, cache_ref, c_in, n_in, m_in,
                  wconv, wqkvo, wg, wgc, wgri, wgrf, bgc, bgr, wdw, bdw, lnag, lnab,
                  waout, hng, wbout, wout, ln1g, ln1b,
                  x1_ref, conv_out, c_out, n_out, m_out,
                  xb_scr, cb0, cb1, cb2, cb3, dw_scr, yain_scr, zq_scr, zg_scr, ybuf):
    j = pl.program_id(1)
    carried = nj > 1
    Lc = nc * L
    T = ns * Lc
    lo = HIST_ROWS - CONV_BUF
    quarter = D_MODEL // 4

    n_blk = D_MODEL // LANES
    per_q = quarter // LANES
    windows = (cb0, cb1, cb2, cb3)

    def window(b):
        return windows[b // per_q]

    def load_history():
        for s in range(ns):
            for b in range(n_blk):
                window(b)[s, b % per_q, lo:HIST_ROWS, :] = cache_ref[s, :, b * LANES:(b + 1) * LANES]

    if carried:
        @pl.when(j == 0)
        def _():
            c_out[...] = c_in[...]
            n_out[...] = n_in[...]
            m_out[...] = m_in[...]
            load_history()
    else:
        load_history()

    def zero_tile(tile):
        bits = pltpu.bitcast(tile, jnp.uint32)
        zero = lax.shift_right_logical(lax.shift_right_logical(bits, jnp.uint32(16)), jnp.uint32(16))
        return pltpu.bitcast(zero, F32)

    def zero_after(tile):
        return zero_tile(tile)[0:1]

    xb_scr[...] = x_ref[...].astype(BF16)

    def glu_quarter(i):
        cs = slice(i * quarter, (i + 1) * quarter)
        gs = slice(D_MODEL + i * quarter, D_MODEL + (i + 1) * quarter)
        xb = xb_scr[...]
        glu = _mm(xb, _w(wconv, cs)) * _sigmoid(_mm(xb, _w(wconv, gs)))
        for s in range(ns):
            for b in range(quarter // LANES):
                windows[i][s, b, HIST_ROWS:HIST_ROWS + Lc, :] = (
                    glu[s * Lc:(s + 1) * Lc, b * LANES:(b + 1) * LANES])
        return windows[i][ns - 1, per_q - 1,
                       HIST_ROWS + Lc - SUBLANES:HIST_ROWS + Lc, :]

    rb = min(CONV_ROW_BLOCK, Lc)

    def conv_block(i, after=None):
        cs = slice(i * LANES, (i + 1) * LANES)
        bias = bdw[:, cs] if after is None else bdw[:, cs] + zero_after(after)
        for s in range(ns):
            for r0 in range(0, Lc, rb):
                acc = jnp.broadcast_to(bias, (rb, LANES))
                for t in range(CONV_WIDTH):
                    wt = jnp.broadcast_to(wdw[t:t + 1, cs], (SUBLANES, LANES))
                    if t > 0:
                        wt = wt + zero_tile(acc[0:SUBLANES])
                    win = window(i)[s, i % per_q, lo + r0 + t:lo + r0 + t + rb, :]
                    acc = acc + (win.reshape(rb // SUBLANES, SUBLANES, LANES) * wt).reshape(rb, LANES)
                dw_scr[s * Lc + r0:s * Lc + r0 + rb, cs] = acc

    def conv_tail():
        for s in range(ns):
            for b in range(n_blk):
                tail = window(b)[s, b % per_q, HIST_ROWS + Lc - CONV_BUF:HIST_ROWS + Lc, :]
                conv_out[s, :, b * LANES:(b + 1) * LANES] = tail
                if carried:
                    window(b)[s, b % per_q, lo:HIST_ROWS, :] = tail

    def ln_swish(r0, r1, after):
        y = _layer_norm(dw_scr[r0:r1, :], lnag[...], lnab[...] + zero_after(after)[:, 0:1])
        yain_scr[r0:r1, :] = (y * _sigmoid(y)).astype(BF16)

    def gate_activations():
        xb = xb_scr[...]
        zgc = _mm(xb, _w(wgc)) + bgc[...]
        ig_col = zgc[:, :LANES]
        lf_col = _log_sigmoid(zgc[:, LANES:])
        ig_row = _mm_nt(wgri[...], xb) + bgr[:, 0:1]
        lf_row = _log_sigmoid(_mm_nt(wgrf[...], xb) + bgr[:, 1:2])
        return ig_col, lf_col, ig_row, lf_row

    def gate_cumsums(ig_col, lf_col, ig_row, lf_row):
        rows = lax.broadcasted_iota(jnp.int32, (L, L), 0)
        cols = lax.broadcasted_iota(jnp.int32, (L, L), 1)
        causal = rows >= cols
        tri_l = causal.astype(F32)
        tri_u = (rows <= cols).astype(F32)
        per_chunk = []
        for ci in range(ns * nc):
            sl = slice(ci * L, (ci + 1) * L)
            b_col = _ones_mm(tri_l, lf_col[sl])
            b_row = _mm_ones(lf_row[:, sl], tri_u)
            per_chunk.append((ig_col[sl], ig_row[:, sl], b_col, b_row))
        return causal, per_chunk

    def qkvo_head(h):
        hs = slice(h * D_MODEL, (h + 1) * D_MODEL)
        zq_scr[:, hs] = _mm(xb_scr[...], _w(wqkvo, hs))
        return zq_scr[T - SUBLANES:T, (h + 1) * D_MODEL - LANES:(h + 1) * D_MODEL]

    lane = lax.broadcasted_iota(jnp.int32, (1, LANES), 1)
    scale = HEAD_DIM ** -0.5

    def mlstm_head(s, c, h, causal, chunk_gates, after):
        ig_col, ig_row, b_col, b_row = chunk_gates
        sl = slice((s * nc + c) * L, (s * nc + c + 1) * L)
        first = c == 0
        c_prev, n_prev, m_prev = ((c_out, n_out, m_out) if carried or not first
                                  else (c_in, n_in, m_in))
        b_c = b_col[:, h:h + 1] + zero_after(after)[:, 0:1]
        ig_c = ig_col[:, h:h + 1]
        b_r = b_row[h:h + 1, :]
        ig_r = ig_row[h:h + 1, :]
        m0 = m_prev[s][:, h:h + 1]
        a_c = b_c + m0
        dm = jnp.where(causal, b_c + (ig_r - b_r), MASKED)
        m_c = jnp.maximum(a_c, jnp.max(dm, axis=1, keepdims=True))
        w_intra = jnp.exp(dm - m_c)
        w_inter = jnp.exp(a_c - m_c)

        base = h * D_MODEL
        q = zq_scr[sl, base:base + HEAD_DIM]
        k = zq_scr[sl, base + HEAD_DIM:base + 2 * HEAD_DIM] * scale
        v = zq_scr[sl, base + 2 * HEAD_DIM:base + 3 * HEAD_DIM]
        o = zq_scr[sl, base + 3 * HEAD_DIM:base + 4 * HEAD_DIM]
        qb = q.astype(BF16)
        vb = v.astype(BF16)
        c0 = c_prev[s, h]
        n0 = n_prev[s, h]

        smat = _mm_nt(qb, k.astype(BF16)) * w_intra
        num = _mm(smat.astype(BF16), vb) + w_inter * _mm(qb, c0.astype(BF16))
        den = (jnp.sum(smat, axis=1, keepdims=True)
               + w_inter * jnp.sum(q * n0, axis=1, keepdims=True))
        hh = num / jnp.maximum(jnp.abs(den), jnp.exp(-m_c))

        m_end = m_c[L - 1:L]
        w_end = jnp.exp(b_c[L - 1:L] - b_c + ig_c - m_end)
        decay = jnp.exp(a_c[L - 1:L] - m_end)
        kw = k * w_end
        c_out[s, h] = decay * c0 + _mm_tn(kw.astype(BF16), vb)
        n_out[s, h] = decay * n0 + jnp.sum(kw, axis=0, keepdims=True)
        m_out[s] = jnp.where(lane == h, m_end, m_out[s] if (carried or not first or h > 0)
                             else m_in[s])

        mu = jnp.mean(hh, axis=1, keepdims=True)
        hc = hh - mu
        var = jnp.mean(hc * hc, axis=1, keepdims=True)
        hn = hc * lax.rsqrt(var + LN_EPS) * hng[:, h * HEAD_DIM:(h + 1) * HEAD_DIM]
        ybuf[sl, h * HEAD_DIM:(h + 1) * HEAD_DIM] = (_sigmoid(o) * hn).astype(BF16)

    def merge_gates_quarter(i):
        cs = slice(i * quarter, (i + 1) * quarter)
        gs = slice(D_MODEL + i * quarter, D_MODEL + (i + 1) * quarter)
        xb = xb_scr[...]
        zg_scr[:, cs] = _sigmoid(_mm(xb, _w(wg, cs)))
        zg_scr[:, gs] = _sigmoid(_mm(xb, _w(wg, gs)))
        return zg_scr[T - SUBLANES:T, D_MODEL + (i + 1) * quarter - LANES:D_MODEL + (i + 1) * quarter]

    a0 = glu_quarter(0)
    gate_act = gate_activations()
    conv_block(0)
    a1 = glu_quarter(1)
    causal, chunk_gates = gate_cumsums(*gate_act)
    conv_block(1, after=a0)
    a2 = glu_quarter(2)
    conv_block(2, after=a1)
    a3 = glu_quarter(3)
    conv_block(3, after=a2)
    b0 = qkvo_head(0)
    conv_block(4, after=a3)
    conv_block(5, after=a3)
    b1 = qkvo_head(1)
    conv_block(6, after=b0)
    conv_block(7, after=b0)
    conv_tail()
    b2 = qkvo_head(2)

    def heads(h, after):
        for s in range(ns):
            for c in range(nc):
                mlstm_head(s, c, h, causal, chunk_gates[s * nc + c], after)

    half = T // 2
    heads(0, b1)
    b3 = qkvo_head(3)
    heads(1, b2)
    merge_gates_quarter(0)
    c1 = merge_gates_quarter(1)
    ln_swish(0, half, b3)
    heads(2, b3)
    merge_gates_quarter(2)
    merge_gates_quarter(3)
    ln_swish(half, T, c1)
    ya_gated = zg_scr[:, :D_MODEL] * _mm(yain_scr[...], _w(waout))
    heads(3, c1)

    y_b = _mm(ybuf[...], _w(wbout))
    merged = (ya_gated + zg_scr[:, D_MODEL:] * y_b).astype(BF16)
    res = ALPHA * x_ref[...] + _mm(merged, _w(wout))
    x1_ref[...] = _layer_norm(res, ln1g[...], ln1b[...])


def _ffn_kernel(x1_ref, wff1, wff2, ln2g, ln2b, out_ref):
    x1 = x1_ref[...]
    hid = jnp.maximum(_mm(x1.astype(BF16), _w(wff1)), 0.0)
    ff = _mm((hid * hid).astype(BF16), _w(wff2))
    out_ref[...] = _layer_norm(ALPHA * x1 + ff, ln2g[...], ln2b[...])


def _resident(shape):
    nd = len(shape)
    return pl.BlockSpec(shape, lambda *_: (0,) * nd, pipeline_mode=pl.Buffered(1))


def _mixer(x2d, cache, c0, n0, m0, weights, *, ns, nc, L, nj):
    Lc = nc * L
    T = ns * Lc
    n_tok = x2d.shape[0]
    G = n_tok // (T * nj)
    n_seq = G * ns
    state_specs = [
        pl.BlockSpec((ns, CONV_BUF, D_MODEL), lambda g, j: (g, 0, 0)),
        pl.BlockSpec((ns, N_HEADS, HEAD_DIM, HEAD_DIM), lambda g, j: (g, 0, 0, 0)),
        pl.BlockSpec((ns, N_HEADS, 1, HEAD_DIM), lambda g, j: (g, 0, 0, 0)),
        pl.BlockSpec((ns, 1, LANES), lambda g, j: (g, 0, 0)),
    ]
    state_in_specs = state_specs
    x_spec = pl.BlockSpec((T, D_MODEL), lambda g, j: (g * nj + j, 0))
    return pl.pallas_call(
        functools.partial(_mixer_kernel, ns, nc, L, nj),
        grid=(G, nj),
        in_specs=[x_spec] + state_in_specs + [_resident(w.shape) for w in weights],
        out_specs=[x_spec] + state_specs,
        out_shape=[
            jax.ShapeDtypeStruct((n_tok, D_MODEL), F32),
            jax.ShapeDtypeStruct((n_seq, CONV_BUF, D_MODEL), F32),
            jax.ShapeDtypeStruct((n_seq, N_HEADS, HEAD_DIM, HEAD_DIM), F32),
            jax.ShapeDtypeStruct((n_seq, N_HEADS, 1, HEAD_DIM), F32),
            jax.ShapeDtypeStruct((n_seq, 1, LANES), F32),
        ],
        scratch_shapes=[
            pltpu.VMEM((T, D_MODEL), BF16),
            *[pltpu.VMEM((ns, D_MODEL // 4 // LANES, HIST_ROWS + Lc, LANES), F32)
              for _ in range(4)],
            pltpu.VMEM((T, D_MODEL), F32),
            pltpu.VMEM((T, D_MODEL), BF16),
            pltpu.VMEM((T, 4 * D_MODEL), F32),
            pltpu.VMEM((T, 2 * D_MODEL), F32),
            pltpu.VMEM((T, D_MODEL), BF16),
        ],
        compiler_params=pltpu.CompilerParams(
            dimension_semantics=("arbitrary", "arbitrary"),
            vmem_limit_bytes=VMEM_LIMIT_BYTES),
        name=f"mixer_ns{ns}",
    )(x2d, cache, c0, n0, m0, *weights)


def _ffn(x1, wff1, wff2, ln2g, ln2b, *, tile):
    n_tok = x1.shape[0]
    x_spec = pl.BlockSpec((tile, D_MODEL), lambda i: (i, 0))
    weights = (wff1, wff2, ln2g, ln2b)
    return pl.pallas_call(
        _ffn_kernel,
        grid=(n_tok // tile,),
        in_specs=[x_spec] + [_resident(w.shape) for w in weights],
        out_specs=x_spec,
        out_shape=jax.ShapeDtypeStruct((n_tok, D_MODEL), F32),
        compiler_params=pltpu.CompilerParams(
            dimension_semantics=("arbitrary",),
            vmem_limit_bytes=VMEM_LIMIT_BYTES),
        name="ffn",
    )(x1, *weights)


def _pad_m(m):
    return jnp.pad(m.astype(F32), ((0, 0), (0, LANES - N_HEADS)))[:, None, :]


def kernel(x_prompt, x_sample, cache_conv, state_C, state_n, state_m, w_in, b_gate, w_dw, b_dw,
           ln_a_g, ln_a_b, w_a_out, hn_g, w_b_out, w_out, ln1_g, ln1_b, w_ff1, w_ff2, ln2_g, ln2_b):
    assert w_in.shape[0] == DEPTH == 1
    bp, sp, _ = x_prompt.shape
    bs, ss, _ = x_sample.shape
    e = D_MODEL
    o_qkvo = 2 * e
    o_gate = o_qkvo + 4 * e
    o_ga = o_gate + 2 * N_HEADS

    def row(v):
        return v[0].astype(F32)[None, :]

    qkvo_blk0 = o_qkvo // HEAD_DIM

    def qkvo_block(j):
        return qkvo_blk0 + (j % 4) * N_HEADS + j // 4

    wt = jnp.swapaxes(w_in, 1, 2)[0]
    w_i_t = wt[o_gate:o_gate + N_HEADS]
    w_f_t = wt[o_gate + N_HEADS:o_ga]
    zpad = jnp.zeros((e, LANES - N_HEADS), wt.dtype)
    wgc = jnp.concatenate([w_i_t.T, zpad, w_f_t.T, zpad], axis=1)
    rpad = jnp.zeros((SUBLANES - N_HEADS, e), wt.dtype)
    wgri = jnp.concatenate([w_i_t, rpad], axis=0).astype(BF16)
    wgrf = jnp.concatenate([w_f_t, rpad], axis=0).astype(BF16)
    bg = b_gate[0].astype(F32)
    bpad = jnp.zeros((LANES - N_HEADS,), F32)
    bgc = jnp.concatenate([bg[:N_HEADS], bpad, bg[N_HEADS:], bpad])[None, :]
    bgr = jnp.pad(bg.reshape(2, N_HEADS).T, ((0, SUBLANES - N_HEADS), (0, 0)))

    weights = (
        _pack_rows_t(wt, cols=o_qkvo),
        _pack_rows_t(wt, cols=4 * e, col_block=HEAD_DIM, col_map=qkvo_block),
        _pack_rows_t(wt[o_ga:]),
        _pack_rows(wgc, col_block=2 * LANES), wgri, wgrf, bgc, bgr,
        w_dw[0].astype(F32), row(b_dw), row(ln_a_g), row(ln_a_b),
        _pack_rows(w_a_out), row(hn_g), _pack_rows(w_b_out), _pack_rows(w_out),
        row(ln1_g), row(ln1_b),
    )
    ffn_w = (_pack_rows(w_ff1), _pack_rows(w_ff2), row(ln2_g), row(ln2_b))

    Lp = MXU_DIM
    ncp = 1
    x1p, convp, cp, np_, mp = _mixer(
        x_prompt.reshape(bp * sp, e),
        jnp.zeros((bp, CONV_BUF, e), F32),
        jnp.zeros((bp, N_HEADS, HEAD_DIM, HEAD_DIM), F32),
        jnp.zeros((bp, N_HEADS, 1, HEAD_DIM), F32),
        jnp.zeros((bp, 1, LANES), F32),
        weights, ns=1, nc=ncp, L=Lp, nj=sp // (ncp * Lp))
    x1s, convs, cs, ns_, ms = _mixer(
        x_sample.reshape(bs * ss, e),
        cache_conv[0].astype(F32),
        state_C[0].astype(F32),
        state_n[0].astype(F32)[:, :, None, :],
        _pad_m(state_m[0]),
        weights, ns=4, nc=1, L=ss, nj=1)

    yp = _ffn(x1p, *ffn_w, tile=512).reshape(bp, sp, e)
    ys = _ffn(x1s, *ffn_w, tile=512).reshape(bs, ss, e)

    def st(conv, c, n, m):
        return conv[None], c[None], n[:, :, 0, :][None], m[:, 0, :N_HEADS][None]

    return (yp, ys) + st(convp, cp, np_, mp) + st(convs, cs, ns_, ms)
```

```python
import functools

import jax
import jax.numpy as jnp
from jax import lax
from jax.experimental import pallas as pl
from jax.experimental.pallas import tpu as pltpu

D_MODEL = 1024
N_HEADS = 4
HEAD_DIM = 256
CONV_WIDTH = 31
CONV_BUF = CONV_WIDTH - 1
D_FF = 4 * D_MODEL
DEPTH = 1
ALPHA = (2.0 * DEPTH) ** 0.25
LN_EPS = 1e-5

LANES = 128
SUBLANES = 8
MXU_DIM = 256
HIST_ROWS = 32
CONV_ROW_BLOCK = 256
MASKED = -1e30
VMEM_LIMIT_BYTES = 58 * 1024 * 1024

F32 = jnp.float32
BF16 = jnp.bfloat16


def _mm(a, b):
    return jnp.dot(a, b, preferred_element_type=F32)


def _mm_nt(a, b):
    return lax.dot_general(a, b, (((1,), (1,)), ((), ())), preferred_element_type=F32)


def _mm_tn(a, b):
    return lax.dot_general(a, b, (((0,), (0,)), ((), ())), preferred_element_type=F32)


def _bf16_terms(x):
    hi = x.astype(BF16)
    r1 = x - hi.astype(F32)
    mid = r1.astype(BF16)
    lo = (r1 - mid.astype(F32)).astype(BF16)
    return hi, mid, lo


def _ones_mm(ones, x):
    ob = ones.astype(BF16)
    hi, mid, lo = _bf16_terms(x)
    return _mm(ob, hi) + _mm(ob, mid) + _mm(ob, lo)


def _mm_ones(x, ones):
    ob = ones.astype(BF16)
    hi, mid, lo = _bf16_terms(x)
    return _mm(hi, ob) + _mm(mid, ob) + _mm(lo, ob)


PACK_ROWS = 1024
PACK_COLS = 512


def _pack_kernel(w_ref, o_ref):
    o_ref[...] = pltpu.bitcast(w_ref[...].astype(BF16), jnp.uint32)


def _pack_rows(w, cols=None, col_block=PACK_COLS, col_map=lambda j: j):
    if w.ndim == 2:
        w = w[None]
    _, k, n = w.shape
    cols = n if cols is None else cols
    kb = min(PACK_ROWS, k)
    return pl.pallas_call(
        _pack_kernel,
        grid=(k // kb, cols // col_block),
        in_specs=[pl.BlockSpec((None, kb, col_block), lambda i, j: (0, i, col_map(j)))],
        out_specs=pl.BlockSpec((kb // 2, col_block), lambda i, j: (i, j)),
        out_shape=jax.ShapeDtypeStruct((k // 2, cols), jnp.uint32),
        compiler_params=pltpu.CompilerParams(dimension_semantics=("arbitrary", "arbitrary")),
        name="pack_weight",
    )(w)


def _pack_t_kernel(wt_ref, o_ref):
    o_ref[...] = pltpu.bitcast(wt_ref[...].T.astype(BF16), jnp.uint32)


def _pack_rows_t(wt, cols=None, col_block=PACK_COLS, col_map=lambda j: j):
    n, k = wt.shape
    cols = n if cols is None else cols
    return pl.pallas_call(
        _pack_t_kernel,
        grid=(cols // col_block,),
        in_specs=[pl.BlockSpec((col_block, k), lambda j: (col_map(j), 0))],
        out_specs=pl.BlockSpec((k // 2, col_block), lambda j: (0, j)),
        out_shape=jax.ShapeDtypeStruct((k // 2, cols), jnp.uint32),
        compiler_params=pltpu.CompilerParams(dimension_semantics=("arbitrary",)),
        name="pack_weight_t",
    )(wt)


def _w(ref, cols=slice(None)):
    return pltpu.bitcast(ref[:, cols], BF16)


def _sigmoid(x):
    return 0.5 * (jnp.tanh(0.5 * x) + 1.0)


def _log_sigmoid(x):
    return jnp.minimum(x, 0.0) - jnp.log(1.0 + jnp.exp(-jnp.abs(x)))


def _layer_norm(x, g, b):
    mu = jnp.mean(x, axis=-1, keepdims=True)
    xc = x - mu
    var = jnp.mean(xc * xc, axis=-1, keepdims=True)
    return xc * lax.rsqrt(var + LN_EPS) * g + b


def _mixer_kernel(ns, nc, L, nj,
                  x_ref, cache_ref, c_in, n_in, m_in,
                  wconv, wqkvo, wg, wgc, wgri, wgrf, bgc, bgr, wdw, bdw, lnag, lnab,
                  waout, hng, wbout, wout, ln1g, ln1b,
                  x1_ref, conv_out, c_out, n_out, m_out,
                  xb_scr, cb0, cb1, cb2, cb3, dw_scr, yain_scr, zq_scr, zg_scr, ybuf):
    j = pl.program_id(1)
    carried = nj > 1
    Lc = nc * L
    T = ns * Lc
    lo = HIST_ROWS - CONV_BUF
    quarter = D_MODEL // 4

    n_blk = D_MODEL // LANES
    per_q = quarter // LANES
    windows = (cb0, cb1, cb2, cb3)

    def window(b):
        return windows[b // per_q]

    def load_history():
        for s in range(ns):
            for b in range(n_blk):
                window(b)[s, b % per_q, lo:HIST_ROWS, :] = cache_ref[s, :, b * LANES:(b + 1) * LANES]

    if carried:
        @pl.when(j == 0)
        def _():
            c_out[...] = c_in[...]
            n_out[...] = n_in[...]
            m_out[...] = m_in[...]
            load_history()
    else:
        load_history()

    def zero_tile(tile):
        bits = pltpu.bitcast(tile, jnp.uint32)
        zero = lax.shift_right_logical(lax.shift_right_logical(bits, jnp.uint32(16)), jnp.uint32(16))
        return pltpu.bitcast(zero, F32)

    def zero_after(tile):
        return zero_tile(tile)[0:1]

    xb_scr[...] = x_ref[...].astype(BF16)

    def glu_quarter(i):
        cs = slice(i * quarter, (i + 1) * quarter)
        gs = slice(D_MODEL + i * quarter, D_MODEL + (i + 1) * quarter)
        xb = xb_scr[...]
        glu = _mm(xb, _w(wconv, cs)) * _sigmoid(_mm(xb, _w(wconv, gs)))
        for s in range(ns):
            for b in range(quarter // LANES):
                windows[i][s, b, HIST_ROWS:HIST_ROWS + Lc, :] = (
                    glu[s * Lc:(s + 1) * Lc, b * LANES:(b + 1) * LANES])
        return windows[i][ns - 1, per_q - 1,
                       HIST_ROWS + Lc - SUBLANES:HIST_ROWS + Lc, :]

    rb = min(CONV_ROW_BLOCK, Lc)

    def conv_block(i, after=None):
        cs = slice(i * LANES, (i + 1) * LANES)
        bias = bdw[:, cs] if after is None else bdw[:, cs] + zero_after(after)
        group = max(1, min(ns, CONV_ROW_BLOCK // rb))
        for s0 in range(0, ns, group):
            for r0 in range(0, Lc, rb):
                accs = [jnp.broadcast_to(bias, (rb, LANES))] * group
                for t in range(CONV_WIDTH):
                    wt = jnp.broadcast_to(wdw[t:t + 1, cs], (SUBLANES, LANES))
                    if t > 0:
                        wt = wt + zero_tile(accs[0][0:SUBLANES])
                    for g in range(group):
                        win = window(i)[s0 + g, i % per_q, lo + r0 + t:lo + r0 + t + rb, :]
                        accs[g] = accs[g] + (win.reshape(rb // SUBLANES, SUBLANES, LANES) * wt
                                             ).reshape(rb, LANES)
                for g in range(group):
                    dw_scr[(s0 + g) * Lc + r0:(s0 + g) * Lc + r0 + rb, cs] = accs[g]

    def conv_tail():
        for s in range(ns):
            for b in range(n_blk):
                tail = window(b)[s, b % per_q, HIST_ROWS + Lc - CONV_BUF:HIST_ROWS + Lc, :]
                conv_out[s, :, b * LANES:(b + 1) * LANES] = tail
                if carried:
                    window(b)[s, b % per_q, lo:HIST_ROWS, :] = tail

    def ln_swish(r0, r1, after):
        y = _layer_norm(dw_scr[r0:r1, :], lnag[...], lnab[...] + zero_after(after)[:, 0:1])
        yain_scr[r0:r1, :] = (y * _sigmoid(y)).astype(BF16)

    def gate_activations():
        xb = xb_scr[...]
        zgc = _mm(xb, _w(wgc)) + bgc[...]
        ig_col = zgc[:, :LANES]
        lf_col = _log_sigmoid(zgc[:, LANES:])
        ig_row = _mm_nt(wgri[...], xb) + bgr[:, 0:1]
        lf_row = _log_sigmoid(_mm_nt(wgrf[...], xb) + bgr[:, 1:2])
        return ig_col, lf_col, ig_row, lf_row

    def gate_cumsums(ig_col, lf_col, ig_row, lf_row):
        rows = lax.broadcasted_iota(jnp.int32, (L, L), 0)
        cols = lax.broadcasted_iota(jnp.int32, (L, L), 1)
        causal = rows >= cols
        tri_l = causal.astype(F32)
        tri_u = (rows <= cols).astype(F32)
        per_chunk = []
        for ci in range(ns * nc):
            sl = slice(ci * L, (ci + 1) * L)
            b_col = _ones_mm(tri_l, lf_col[sl])
            b_row = _mm_ones(lf_row[:, sl], tri_u)
            per_chunk.append((ig_col[sl], ig_row[:, sl], b_col, b_row))
        return causal, per_chunk

    def qkvo_head(h):
        hs = slice(h * D_MODEL, (h + 1) * D_MODEL)
        zq_scr[:, hs] = _mm(xb_scr[...], _w(wqkvo, hs))
        return zq_scr[T - SUBLANES:T, (h + 1) * D_MODEL - LANES:(h + 1) * D_MODEL]

    lane = lax.broadcasted_iota(jnp.int32, (1, LANES), 1)
    scale = HEAD_DIM ** -0.5

    def mlstm_head(s, c, h, causal, chunk_gates, after):
        ig_col, ig_row, b_col, b_row = chunk_gates
        sl = slice((s * nc + c) * L, (s * nc + c + 1) * L)
        first = c == 0
        c_prev, n_prev, m_prev = ((c_out, n_out, m_out) if carried or not first
                                  else (c_in, n_in, m_in))
        b_c = b_col[:, h:h + 1]
        if ns == 1:
            b_c = b_c + zero_after(after)[:, 0:1]
        ig_c = ig_col[:, h:h + 1]
        b_r = b_row[h:h + 1, :]
        ig_r = ig_row[h:h + 1, :]
        m0 = m_prev[s][:, h:h + 1]
        a_c = b_c + m0
        dm = jnp.where(causal, b_c + (ig_r - b_r), MASKED)
        m_c = jnp.maximum(a_c, jnp.max(dm, axis=1, keepdims=True))
        w_intra = jnp.exp(dm - m_c)
        w_inter = jnp.exp(a_c - m_c)

        base = h * D_MODEL
        q = zq_scr[sl, base:base + HEAD_DIM]
        k = zq_scr[sl, base + HEAD_DIM:base + 2 * HEAD_DIM] * scale
        v = zq_scr[sl, base + 2 * HEAD_DIM:base + 3 * HEAD_DIM]
        o = zq_scr[sl, base + 3 * HEAD_DIM:base + 4 * HEAD_DIM]
        qb = q.astype(BF16)
        vb = v.astype(BF16)
        c0 = c_prev[s, h]
        n0 = n_prev[s, h]

        smat = _mm_nt(qb, k.astype(BF16)) * w_intra
        num = _mm(smat.astype(BF16), vb) + w_inter * _mm(qb, c0.astype(BF16))
        den = (jnp.sum(smat, axis=1, keepdims=True)
               + w_inter * jnp.sum(q * n0, axis=1, keepdims=True))
        hh = num / jnp.maximum(jnp.abs(den), jnp.exp(-m_c))

        m_end = m_c[L - 1:L]
        w_end = jnp.exp(b_c[L - 1:L] - b_c + ig_c - m_end)
        decay = jnp.exp(a_c[L - 1:L] - m_end)
        kw = k * w_end
        c_out[s, h] = decay * c0 + _mm_tn(kw.astype(BF16), vb)
        n_out[s, h] = decay * n0 + jnp.sum(kw, axis=0, keepdims=True)
        m_out[s] = jnp.where(lane == h, m_end, m_out[s] if (carried or not first or h > 0)
                             else m_in[s])

        mu = jnp.mean(hh, axis=1, keepdims=True)
        hc = hh - mu
        var = jnp.mean(hc * hc, axis=1, keepdims=True)
        hn = hc * lax.rsqrt(var + LN_EPS) * hng[:, h * HEAD_DIM:(h + 1) * HEAD_DIM]
        ybuf[sl, h * HEAD_DIM:(h + 1) * HEAD_DIM] = (_sigmoid(o) * hn).astype(BF16)

    def merge_gates_quarter(i):
        cs = slice(i * quarter, (i + 1) * quarter)
        gs = slice(D_MODEL + i * quarter, D_MODEL + (i + 1) * quarter)
        xb = xb_scr[...]
        zg_scr[:, cs] = _sigmoid(_mm(xb, _w(wg, cs)))
        zg_scr[:, gs] = _sigmoid(_mm(xb, _w(wg, gs)))
        return zg_scr[T - SUBLANES:T, D_MODEL + (i + 1) * quarter - LANES:D_MODEL + (i + 1) * quarter]

    a0 = glu_quarter(0)
    conv_block(0)
    conv_block(1)
    gate_act = gate_activations()
    a1 = glu_quarter(1)
    conv_block(2)
    conv_block(3)
    causal, chunk_gates = gate_cumsums(*gate_act)
    a2 = glu_quarter(2)
    conv_block(4)
    a3 = glu_quarter(3)
    conv_block(5, after=a1)
    b0 = qkvo_head(0)
    conv_block(6, after=a2)
    b1 = qkvo_head(1)
    conv_block(7, after=b0)
    conv_tail()
    b2 = qkvo_head(2)

    def heads(h, after):
        for s in range(ns):
            for c in range(nc):
                mlstm_head(s, c, h, causal, chunk_gates[s * nc + c], after)

    half = T // 2
    heads(0, b1)
    ln_swish(0, half, b2)
    b3 = qkvo_head(3)
    heads(1, b2)
    ln_swish(half, T, b3)
    merge_gates_quarter(0)
    c1 = merge_gates_quarter(1)
    heads(2, b3)
    merge_gates_quarter(2)
    merge_gates_quarter(3)
    ya_gated = zg_scr[:, :D_MODEL] * _mm(yain_scr[...], _w(waout))
    heads(3, c1)

    y_b = _mm(ybuf[...], _w(wbout))
    merged = (ya_gated + zg_scr[:, D_MODEL:] * y_b).astype(BF16)
    res = ALPHA * x_ref[...] + _mm(merged, _w(wout))
    x1_ref[...] = _layer_norm(res, ln1g[...], ln1b[...])


def _ffn_kernel(x1_ref, wff1, wff2, ln2g, ln2b, out_ref):
    x1 = x1_ref[...]
    hid = jnp.maximum(_mm(x1.astype(BF16), _w(wff1)), 0.0)
    ff = _mm((hid * hid).astype(BF16), _w(wff2))
    out_ref[...] = _layer_norm(ALPHA * x1 + ff, ln2g[...], ln2b[...])


def _resident(shape):
    nd = len(shape)
    return pl.BlockSpec(shape, lambda *_: (0,) * nd, pipeline_mode=pl.Buffered(1))


def _mixer(x2d, cache, c0, n0, m0, weights, *, ns, nc, L, nj):
    Lc = nc * L
    T = ns * Lc
    n_tok = x2d.shape[0]
    G = n_tok // (T * nj)
    n_seq = G * ns
    state_specs = [
        pl.BlockSpec((ns, CONV_BUF, D_MODEL), lambda g, j: (g, 0, 0)),
        pl.BlockSpec((ns, N_HEADS, HEAD_DIM, HEAD_DIM), lambda g, j: (g, 0, 0, 0)),
        pl.BlockSpec((ns, N_HEADS, 1, HEAD_DIM), lambda g, j: (g, 0, 0, 0)),
        pl.BlockSpec((ns, 1, LANES), lambda g, j: (g, 0, 0)),
    ]
    state_in_specs = state_specs
    x_spec = pl.BlockSpec((T, D_MODEL), lambda g, j: (g * nj + j, 0))
    return pl.pallas_call(
        functools.partial(_mixer_kernel, ns, nc, L, nj),
        grid=(G, nj),
        in_specs=[x_spec] + state_in_specs + [_resident(w.shape) for w in weights],
        out_specs=[x_spec] + state_specs,
        out_shape=[
            jax.ShapeDtypeStruct((n_tok, D_MODEL), F32),
            jax.ShapeDtypeStruct((n_seq, CONV_BUF, D_MODEL), F32),
            jax.ShapeDtypeStruct((n_seq, N_HEADS, HEAD_DIM, HEAD_DIM), F32),
            jax.ShapeDtypeStruct((n_seq, N_HEADS, 1, HEAD_DIM), F32),
            jax.ShapeDtypeStruct((n_seq, 1, LANES), F32),
        ],
        scratch_shapes=[
            pltpu.VMEM((T, D_MODEL), BF16),
            *[pltpu.VMEM((ns, D_MODEL // 4 // LANES, HIST_ROWS + Lc, LANES), F32)
              for _ in range(4)],
            pltpu.VMEM((T, D_MODEL), F32),
            pltpu.VMEM((T, D_MODEL), BF16),
            pltpu.VMEM((T, 4 * D_MODEL), F32),
            pltpu.VMEM((T, 2 * D_MODEL), F32),
            pltpu.VMEM((T, D_MODEL), BF16),
        ],
        compiler_params=pltpu.CompilerParams(
            dimension_semantics=("arbitrary", "arbitrary"),
            vmem_limit_bytes=VMEM_LIMIT_BYTES),
        name=f"mixer_ns{ns}",
    )(x2d, cache, c0, n0, m0, *weights)


def _ffn(x1, wff1, wff2, ln2g, ln2b, *, tile):
    n_tok = x1.shape[0]
    x_spec = pl.BlockSpec((tile, D_MODEL), lambda i: (i, 0))
    weights = (wff1, wff2, ln2g, ln2b)
    return pl.pallas_call(
        _ffn_kernel,
        grid=(n_tok // tile,),
        in_specs=[x_spec] + [_resident(w.shape) for w in weights],
        out_specs=x_spec,
        out_shape=jax.ShapeDtypeStruct((n_tok, D_MODEL), F32),
        compiler_params=pltpu.CompilerParams(
            dimension_semantics=("arbitrary",),
            vmem_limit_bytes=VMEM_LIMIT_BYTES),
        name="ffn",
    )(x1, *weights)


def _pad_m(m):
    return jnp.pad(m.astype(F32), ((0, 0), (0, LANES - N_HEADS)))[:, None, :]


def kernel(x_prompt, x_sample, cache_conv, state_C, state_n, state_m, w_in, b_gate, w_dw, b_dw,
           ln_a_g, ln_a_b, w_a_out, hn_g, w_b_out, w_out, ln1_g, ln1_b, w_ff1, w_ff2, ln2_g, ln2_b):
    assert w_in.shape[0] == DEPTH == 1
    bp, sp, _ = x_prompt.shape
    bs, ss, _ = x_sample.shape
    e = D_MODEL
    o_qkvo = 2 * e
    o_gate = o_qkvo + 4 * e
    o_ga = o_gate + 2 * N_HEADS

    def row(v):
        return v[0].astype(F32)[None, :]

    qkvo_blk0 = o_qkvo // HEAD_DIM

    def qkvo_block(j):
        return qkvo_blk0 + (j % 4) * N_HEADS + j // 4

    wt = jnp.swapaxes(w_in, 1, 2)[0]
    w_i_t = wt[o_gate:o_gate + N_HEADS]
    w_f_t = wt[o_gate + N_HEADS:o_ga]
    zpad = jnp.zeros((e, LANES - N_HEADS), wt.dtype)
    wgc = jnp.concatenate([w_i_t.T, zpad, w_f_t.T, zpad], axis=1)
    rpad = jnp.zeros((SUBLANES - N_HEADS, e), wt.dtype)
    wgri = jnp.concatenate([w_i_t, rpad], axis=0).astype(BF16)
    wgrf = jnp.concatenate([w_f_t, rpad], axis=0).astype(BF16)
    bg = b_gate[0].astype(F32)
    bpad = jnp.zeros((LANES - N_HEADS,), F32)
    bgc = jnp.concatenate([bg[:N_HEADS], bpad, bg[N_HEADS:], bpad])[None, :]
    bgr = jnp.pad(bg.reshape(2, N_HEADS).T, ((0, SUBLANES - N_HEADS), (0, 0)))

    weights = (
        _pack_rows_t(wt, cols=o_qkvo),
        _pack_rows_t(wt, cols=4 * e, col_block=HEAD_DIM, col_map=qkvo_block),
        _pack_rows_t(wt[o_ga:]),
        _pack_rows(wgc, col_block=2 * LANES), wgri, wgrf, bgc, bgr,
        w_dw[0].astype(F32), row(b_dw), row(ln_a_g), row(ln_a_b),
        _pack_rows(w_a_out), row(hn_g), _pack_rows(w_b_out), _pack_rows(w_out),
        row(ln1_g), row(ln1_b),
    )
    ffn_w = (_pack_rows(w_ff1), _pack_rows(w_ff2), row(ln2_g), row(ln2_b))

    Lp = MXU_DIM
    ncp = 1
    x1p, convp, cp, np_, mp = _mixer(
        x_prompt.reshape(bp * sp, e),
        jnp.zeros((bp, CONV_BUF, e), F32),
        jnp.zeros((bp, N_HEADS, HEAD_DIM, HEAD_DIM), F32),
        jnp.zeros((bp, N_HEADS, 1, HEAD_DIM), F32),
        jnp.zeros((bp, 1, LANES), F32),
        weights, ns=1, nc=ncp, L=Lp, nj=sp // (ncp * Lp))
    x1s, convs, cs, ns_, ms = _mixer(
        x_sample.reshape(bs * ss, e),
        cache_conv[0].astype(F32),
        state_C[0].astype(F32),
        state_n[0].astype(F32)[:, :, None, :],
        _pad_m(state_m[0]),
        weights, ns=4, nc=1, L=ss, nj=1)

    yp = _ffn(x1p, *ffn_w, tile=512).reshape(bp, sp, e)
    ys = _ffn(x1s, *ffn_w, tile=512).reshape(bs, ss, e)

    def st(conv, c, n, m):
        return conv[None], c[None], n[:, :, 0, :][None], m[:, 0, :N_HEADS][None]

    return (yp, ys) + st(convp, cp, np_, mp) + st(convs, cs, ns_, ms)
```

```python
import functools

import jax
import jax.numpy as jnp
from jax import lax
from jax.experimental import pallas as pl
from jax.experimental.pallas import tpu as pltpu

D_MODEL = 1024
N_HEADS = 4
HEAD_DIM = 256
CONV_WIDTH = 31
CONV_BUF = CONV_WIDTH - 1
D_FF = 4 * D_MODEL
DEPTH = 1
ALPHA = (2.0 * DEPTH) ** 0.25
LN_EPS = 1e-5

LANES = 128
SUBLANES = 8
MXU_DIM = 256
HIST_ROWS = 32
CONV_ROW_BLOCK = 256
MASKED = -1e30
VMEM_LIMIT_BYTES = 58 * 1024 * 1024

F32 = jnp.float32
BF16 = jnp.bfloat16


def _mm(a, b):
    return jnp.dot(a, b, preferred_element_type=F32)


def _mm_nt(a, b):
    return lax.dot_general(a, b, (((1,), (1,)), ((), ())), preferred_element_type=F32)


def _mm_tn(a, b):
    return lax.dot_general(a, b, (((0,), (0,)), ((), ())), preferred_element_type=F32)


def _bf16_terms(x):
    hi = x.astype(BF16)
    r1 = x - hi.astype(F32)
    mid = r1.astype(BF16)
    lo = (r1 - mid.astype(F32)).astype(BF16)
    return hi, mid, lo


def _ones_mm(ones, x):
    ob = ones.astype(BF16)
    hi, mid, lo = _bf16_terms(x)
    return _mm(ob, hi) + _mm(ob, mid) + _mm(ob, lo)


def _mm_ones(x, ones):
    ob = ones.astype(BF16)
    hi, mid, lo = _bf16_terms(x)
    return _mm(hi, ob) + _mm(mid, ob) + _mm(lo, ob)


PACK_ROWS = 1024
PACK_COLS = 512


def _pack_kernel(w_ref, o_ref):
    o_ref[...] = pltpu.bitcast(w_ref[...].astype(BF16), jnp.uint32)


def _pack_rows(w, cols=None, col_block=PACK_COLS, col_map=lambda j: j):
    if w.ndim == 2:
        w = w[None]
    _, k, n = w.shape
    cols = n if cols is None else cols
    kb = min(PACK_ROWS, k)
    return pl.pallas_call(
        _pack_kernel,
        grid=(k // kb, cols // col_block),
        in_specs=[pl.BlockSpec((None, kb, col_block), lambda i, j: (0, i, col_map(j)))],
        out_specs=pl.BlockSpec((kb // 2, col_block), lambda i, j: (i, j)),
        out_shape=jax.ShapeDtypeStruct((k // 2, cols), jnp.uint32),
        compiler_params=pltpu.CompilerParams(dimension_semantics=("arbitrary", "arbitrary")),
        name="pack_weight",
    )(w)


def _pack_t_kernel(wt_ref, o_ref):
    o_ref[...] = pltpu.bitcast(wt_ref[...].T.astype(BF16), jnp.uint32)


def _pack_rows_t(wt, cols=None, col_block=PACK_COLS, col_map=lambda j: j, row0=0):
    n, k = wt.shape
    cols = n if cols is None else cols
    if row0:
        in_spec = pl.BlockSpec((pl.Element(col_block), pl.Element(k)),
                               lambda j: (pl.multiple_of(row0 + col_map(j) * col_block, SUBLANES), 0))
    else:
        in_spec = pl.BlockSpec((col_block, k), lambda j: (col_map(j), 0))
    return pl.pallas_call(
        _pack_t_kernel,
        grid=(cols // col_block,),
        in_specs=[in_spec],
        out_specs=pl.BlockSpec((k // 2, col_block), lambda j: (0, j)),
        out_shape=jax.ShapeDtypeStruct((k // 2, cols), jnp.uint32),
        compiler_params=pltpu.CompilerParams(dimension_semantics=("arbitrary",)),
        name="pack_weight_t",
    )(wt)


def _w(ref, cols=slice(None)):
    return pltpu.bitcast(ref[:, cols], BF16)


def _sigmoid(x):
    return 0.5 * (jnp.tanh(0.5 * x) + 1.0)


def _log_sigmoid(x):
    return jnp.minimum(x, 0.0) - jnp.log(1.0 + jnp.exp(-jnp.abs(x)))


def _layer_norm(x, g, b):
    mu = jnp.mean(x, axis=-1, keepdims=True)
    xc = x - mu
    var = jnp.mean(xc * xc, axis=-1, keepdims=True)
    return xc * lax.rsqrt(var + LN_EPS) * g + b


def _mixer_kernel(ns, nc, L, nj,
                  x_ref, cache_ref, c_in, n_in, m_in,
                  wconv, wqkvo, wg, wgc, wgri, wgrf, bgc, bgr, wdw, bdw, lnag, lnab,
                  waout, hng, wbout, wout, ln1g, ln1b,
                  x1_ref, conv_out, c_out, n_out, m_out,
                  xb_scr, cb0, cb1, cb2, cb3, dw_scr, yain_scr, zq_scr, zg_scr, ybuf):
    j = pl.program_id(1)
    carried = nj > 1
    Lc = nc * L
    T = ns * Lc
    lo = HIST_ROWS - CONV_BUF
    quarter = D_MODEL // 4

    n_blk = D_MODEL // LANES
    per_q = quarter // LANES
    windows = (cb0, cb1, cb2, cb3)

    def window(b):
        return windows[b // per_q]

    def load_history():
        for s in range(ns):
            for b in range(n_blk):
                window(b)[s, b % per_q, lo:HIST_ROWS, :] = cache_ref[s, :, b * LANES:(b + 1) * LANES]

    if carried:
        @pl.when(j == 0)
        def _():
            c_out[...] = c_in[...]
            n_out[...] = n_in[...]
            m_out[...] = m_in[...]
            load_history()
    else:
        load_history()

    def zero_tile(tile):
        bits = pltpu.bitcast(tile, jnp.uint32)
        zero = lax.shift_right_logical(lax.shift_right_logical(bits, jnp.uint32(16)), jnp.uint32(16))
        return pltpu.bitcast(zero, F32)

    def zero_after(tile):
        return zero_tile(tile)[0:1]

    xb_scr[...] = x_ref[...].astype(BF16)

    def glu_quarter(i):
        cs = slice(i * quarter, (i + 1) * quarter)
        gs = slice(D_MODEL + i * quarter, D_MODEL + (i + 1) * quarter)
        xb = xb_scr[...]
        glu = _mm(xb, _w(wconv, cs)) * _sigmoid(_mm(xb, _w(wconv, gs)))
        for s in range(ns):
            for b in range(quarter // LANES):
                windows[i][s, b, HIST_ROWS:HIST_ROWS + Lc, :] = (
                    glu[s * Lc:(s + 1) * Lc, b * LANES:(b + 1) * LANES])
        return windows[i][ns - 1, per_q - 1,
                       HIST_ROWS + Lc - SUBLANES:HIST_ROWS + Lc, :]

    rb = min(CONV_ROW_BLOCK, Lc)

    def conv_block(i, after=None):
        cs = slice(i * LANES, (i + 1) * LANES)
        bias = bdw[:, cs] if after is None else bdw[:, cs] + zero_after(after)
        group = max(1, min(ns, CONV_ROW_BLOCK // rb))
        for s0 in range(0, ns, group):
            for r0 in range(0, Lc, rb):
                accs = [jnp.broadcast_to(bias, (rb, LANES))] * group
                for t in range(CONV_WIDTH):
                    wt = jnp.broadcast_to(wdw[t:t + 1, cs], (SUBLANES, LANES))
                    if t > 0:
                        wt = wt + zero_tile(accs[0][0:SUBLANES])
                    for g in range(group):
                        win = window(i)[s0 + g, i % per_q, lo + r0 + t:lo + r0 + t + rb, :]
                        accs[g] = accs[g] + (win.reshape(rb // SUBLANES, SUBLANES, LANES) * wt
                                             ).reshape(rb, LANES)
                for g in range(group):
                    dw_scr[(s0 + g) * Lc + r0:(s0 + g) * Lc + r0 + rb, cs] = accs[g]

    def conv_tail():
        for s in range(ns):
            for b in range(n_blk):
                tail = window(b)[s, b % per_q, HIST_ROWS + Lc - CONV_BUF:HIST_ROWS + Lc, :]
                conv_out[s, :, b * LANES:(b + 1) * LANES] = tail
                if carried:
                    window(b)[s, b % per_q, lo:HIST_ROWS, :] = tail

    def ln_swish(r0, r1, after):
        y = _layer_norm(dw_scr[r0:r1, :], lnag[...], lnab[...] + zero_after(after)[:, 0:1])
        yain_scr[r0:r1, :] = (y * _sigmoid(y)).astype(BF16)

    def gate_activations():
        xb = xb_scr[...]
        zgc = _mm(xb, _w(wgc)) + bgc[...]
        ig_col = zgc[:, :LANES]
        lf_col = _log_sigmoid(zgc[:, LANES:])
        ig_row = _mm_nt(wgri[...], xb) + bgr[:, 0:1]
        lf_row = _log_sigmoid(_mm_nt(wgrf[...], xb) + bgr[:, 1:2])
        return ig_col, lf_col, ig_row, lf_row

    def gate_cumsums(ig_col, lf_col, ig_row, lf_row):
        rows = lax.broadcasted_iota(jnp.int32, (L, L), 0)
        cols = lax.broadcasted_iota(jnp.int32, (L, L), 1)
        causal = rows >= cols
        tri_l = causal.astype(F32)
        tri_u = (rows <= cols).astype(F32)
        per_chunk = []
        for ci in range(ns * nc):
            sl = slice(ci * L, (ci + 1) * L)
            b_col = _ones_mm(tri_l, lf_col[sl])
            b_row = _mm_ones(lf_row[:, sl], tri_u)
            per_chunk.append((ig_col[sl], ig_row[:, sl], b_col, b_row))
        return causal, per_chunk

    def qkvo_head(h):
        hs = slice(h * D_MODEL, (h + 1) * D_MODEL)
        zq_scr[:, hs] = _mm(xb_scr[...], _w(wqkvo, hs))
        return zq_scr[T - SUBLANES:T, (h + 1) * D_MODEL - LANES:(h + 1) * D_MODEL]

    lane = lax.broadcasted_iota(jnp.int32, (1, LANES), 1)
    scale = HEAD_DIM ** -0.5

    def mlstm_head(s, c, h, causal, chunk_gates, after):
        ig_col, ig_row, b_col, b_row = chunk_gates
        sl = slice((s * nc + c) * L, (s * nc + c + 1) * L)
        first = c == 0
        c_prev, n_prev, m_prev = ((c_out, n_out, m_out) if carried or not first
                                  else (c_in, n_in, m_in))
        b_c = b_col[:, h:h + 1]
        if ns == 1:
            b_c = b_c + zero_after(after)[:, 0:1]
        ig_c = ig_col[:, h:h + 1]
        b_r = b_row[h:h + 1, :]
        ig_r = ig_row[h:h + 1, :]
        m0 = m_prev[s][:, h:h + 1]
        a_c = b_c + m0
        dm = jnp.where(causal, b_c + (ig_r - b_r), MASKED)
        m_c = jnp.maximum(a_c, jnp.max(dm, axis=1, keepdims=True))
        w_intra = jnp.exp(dm - m_c)
        w_inter = jnp.exp(a_c - m_c)

        base = h * D_MODEL
        q = zq_scr[sl, base:base + HEAD_DIM]
        k = zq_scr[sl, base + HEAD_DIM:base + 2 * HEAD_DIM] * scale
        v = zq_scr[sl, base + 2 * HEAD_DIM:base + 3 * HEAD_DIM]
        o = zq_scr[sl, base + 3 * HEAD_DIM:base + 4 * HEAD_DIM]
        qb = q.astype(BF16)
        vb = v.astype(BF16)
        c0 = c_prev[s, h]
        n0 = n_prev[s, h]

        smat = _mm_nt(qb, k.astype(BF16)) * w_intra
        num = _mm(smat.astype(BF16), vb) + w_inter * _mm(qb, c0.astype(BF16))
        den = (jnp.sum(smat, axis=1, keepdims=True)
               + w_inter * jnp.sum(q * n0, axis=1, keepdims=True))
        hh = num / jnp.maximum(jnp.abs(den), jnp.exp(-m_c))

        m_end = m_c[L - 1:L]
        w_end = jnp.exp(b_c[L - 1:L] - b_c + ig_c - m_end)
        decay = jnp.exp(a_c[L - 1:L] - m_end)
        kw = k * w_end
        c_out[s, h] = decay * c0 + _mm_tn(kw.astype(BF16), vb)
        n_out[s, h] = decay * n0 + jnp.sum(kw, axis=0, keepdims=True)
        m_out[s] = jnp.where(lane == h, m_end, m_out[s] if (carried or not first or h > 0)
                             else m_in[s])

        mu = jnp.mean(hh, axis=1, keepdims=True)
        hc = hh - mu
        var = jnp.mean(hc * hc, axis=1, keepdims=True)
        hn = hc * lax.rsqrt(var + LN_EPS) * hng[:, h * HEAD_DIM:(h + 1) * HEAD_DIM]
        ybuf[sl, h * HEAD_DIM:(h + 1) * HEAD_DIM] = (_sigmoid(o) * hn).astype(BF16)

    def merge_gates_quarter(i):
        cs = slice(i * quarter, (i + 1) * quarter)
        gs = slice(D_MODEL + i * quarter, D_MODEL + (i + 1) * quarter)
        xb = xb_scr[...]
        zg_scr[:, cs] = _sigmoid(_mm(xb, _w(wg, cs)))
        zg_scr[:, gs] = _sigmoid(_mm(xb, _w(wg, gs)))
        return zg_scr[T - SUBLANES:T, D_MODEL + (i + 1) * quarter - LANES:D_MODEL + (i + 1) * quarter]

    a0 = glu_quarter(0)
    conv_block(0)
    conv_block(1)
    gate_act = gate_activations()
    a1 = glu_quarter(1)
    conv_block(2)
    conv_block(3)
    causal, chunk_gates = gate_cumsums(*gate_act)
    a2 = glu_quarter(2)
    conv_block(4)
    a3 = glu_quarter(3)
    conv_block(5, after=a1)
    b0 = qkvo_head(0)
    conv_block(6, after=a2)
    b1 = qkvo_head(1)
    conv_block(7, after=b0)
    conv_tail()
    b2 = qkvo_head(2)

    def heads(h, after):
        for s in range(ns):
            for c in range(nc):
                mlstm_head(s, c, h, causal, chunk_gates[s * nc + c], after)

    half = T // 2
    heads(0, b1)
    ln_swish(0, half, b2)
    b3 = qkvo_head(3)
    heads(1, b2)
    ln_swish(half, T, b3)
    merge_gates_quarter(0)
    c1 = merge_gates_quarter(1)
    heads(2, b3)
    merge_gates_quarter(2)
    merge_gates_quarter(3)
    ya_gated = zg_scr[:, :D_MODEL] * _mm(yain_scr[...], _w(waout))
    heads(3, c1)

    y_b = _mm(ybuf[...], _w(wbout))
    merged = (ya_gated + zg_scr[:, D_MODEL:] * y_b).astype(BF16)
    res = ALPHA * x_ref[...] + _mm(merged, _w(wout))
    x1_ref[...] = _layer_norm(res, ln1g[...], ln1b[...])


def _ffn_kernel(x1_ref, wff1, wff2, ln2g, ln2b, out_ref):
    rows = x1_ref.shape[0] // 2
    hid = []
    for r in range(2):
        x1 = x1_ref[r * rows:(r + 1) * rows, :]
        h = jnp.maximum(_mm(x1.astype(BF16), _w(wff1)), 0.0)
        hid.append((h * h).astype(BF16))
    for r in range(2):
        x1 = x1_ref[r * rows:(r + 1) * rows, :]
        ff = _mm(hid[r], _w(wff2))
        out_ref[r * rows:(r + 1) * rows, :] = _layer_norm(ALPHA * x1 + ff, ln2g[...], ln2b[...])


def _resident(shape):
    nd = len(shape)
    return pl.BlockSpec(shape, lambda *_: (0,) * nd, pipeline_mode=pl.Buffered(1))


def _mixer(x2d, cache, c0, n0, m0, weights, *, ns, nc, L, nj):
    Lc = nc * L
    T = ns * Lc
    n_tok = x2d.shape[0]
    G = n_tok // (T * nj)
    n_seq = G * ns
    state_specs = [
        pl.BlockSpec((ns, CONV_BUF, D_MODEL), lambda g, j: (g, 0, 0)),
        pl.BlockSpec((ns, N_HEADS, HEAD_DIM, HEAD_DIM), lambda g, j: (g, 0, 0, 0)),
        pl.BlockSpec((ns, N_HEADS, 1, HEAD_DIM), lambda g, j: (g, 0, 0, 0)),
        pl.BlockSpec((ns, 1, LANES), lambda g, j: (g, 0, 0)),
    ]
    state_in_specs = state_specs
    x_spec = pl.BlockSpec((T, D_MODEL), lambda g, j: (g * nj + j, 0))
    return pl.pallas_call(
        functools.partial(_mixer_kernel, ns, nc, L, nj),
        grid=(G, nj),
        in_specs=[x_spec] + state_in_specs + [_resident(w.shape) for w in weights],
        out_specs=[x_spec] + state_specs,
        out_shape=[
            jax.ShapeDtypeStruct((n_tok, D_MODEL), F32),
            jax.ShapeDtypeStruct((n_seq, CONV_BUF, D_MODEL), F32),
            jax.ShapeDtypeStruct((n_seq, N_HEADS, HEAD_DIM, HEAD_DIM), F32),
            jax.ShapeDtypeStruct((n_seq, N_HEADS, 1, HEAD_DIM), F32),
            jax.ShapeDtypeStruct((n_seq, 1, LANES), F32),
        ],
        scratch_shapes=[
            pltpu.VMEM((T, D_MODEL), BF16),
            *[pltpu.VMEM((ns, D_MODEL // 4 // LANES, HIST_ROWS + Lc, LANES), F32)
              for _ in range(4)],
            pltpu.VMEM((T, D_MODEL), F32),
            pltpu.VMEM((T, D_MODEL), BF16),
            pltpu.VMEM((T, 4 * D_MODEL), F32),
            pltpu.VMEM((T, 2 * D_MODEL), F32),
            pltpu.VMEM((T, D_MODEL), BF16),
        ],
        compiler_params=pltpu.CompilerParams(
            dimension_semantics=("arbitrary", "arbitrary"),
            vmem_limit_bytes=VMEM_LIMIT_BYTES),
        name=f"mixer_ns{ns}",
    )(x2d, cache, c0, n0, m0, *weights)


def _ffn(x1, wff1, wff2, ln2g, ln2b, *, tile):
    n_tok = x1.shape[0]
    x_spec = pl.BlockSpec((tile, D_MODEL), lambda i: (i, 0))
    weights = (wff1, wff2, ln2g, ln2b)
    return pl.pallas_call(
        _ffn_kernel,
        grid=(n_tok // tile,),
        in_specs=[x_spec] + [_resident(w.shape) for w in weights],
        out_specs=x_spec,
        out_shape=jax.ShapeDtypeStruct((n_tok, D_MODEL), F32),
        compiler_params=pltpu.CompilerParams(
            dimension_semantics=("arbitrary",),
            vmem_limit_bytes=VMEM_LIMIT_BYTES),
        name="ffn",
    )(x1, *weights)


def _pad_m(m):
    return jnp.pad(m.astype(F32), ((0, 0), (0, LANES - N_HEADS)))[:, None, :]


def kernel(x_prompt, x_sample, cache_conv, state_C, state_n, state_m, w_in, b_gate, w_dw, b_dw,
           ln_a_g, ln_a_b, w_a_out, hn_g, w_b_out, w_out, ln1_g, ln1_b, w_ff1, w_ff2, ln2_g, ln2_b):
    assert w_in.shape[0] == DEPTH == 1
    bp, sp, _ = x_prompt.shape
    bs, ss, _ = x_sample.shape
    e = D_MODEL
    o_qkvo = 2 * e
    o_gate = o_qkvo + 4 * e
    o_ga = o_gate + 2 * N_HEADS

    def row(v):
        return v[0].astype(F32)[None, :]

    qkvo_blk0 = o_qkvo // HEAD_DIM

    def qkvo_block(j):
        return qkvo_blk0 + (j % 4) * N_HEADS + j // 4

    wt = jnp.swapaxes(w_in, 1, 2)[0]
    w_i_t = wt[o_gate:o_gate + N_HEADS]
    w_f_t = wt[o_gate + N_HEADS:o_ga]
    zpad = jnp.zeros((e, LANES - N_HEADS), wt.dtype)
    wgc = jnp.concatenate([w_i_t.T, zpad, w_f_t.T, zpad], axis=1)
    rpad = jnp.zeros((SUBLANES - N_HEADS, e), wt.dtype)
    wgri = jnp.concatenate([w_i_t, rpad], axis=0).astype(BF16)
    wgrf = jnp.concatenate([w_f_t, rpad], axis=0).astype(BF16)
    bg = b_gate[0].astype(F32)
    bpad = jnp.zeros((LANES - N_HEADS,), F32)
    bgc = jnp.concatenate([bg[:N_HEADS], bpad, bg[N_HEADS:], bpad])[None, :]
    bgr = jnp.pad(bg.reshape(2, N_HEADS).T, ((0, SUBLANES - N_HEADS), (0, 0)))

    weights = (
        _pack_rows_t(wt, cols=o_qkvo),
        _pack_rows_t(wt, cols=4 * e, col_block=HEAD_DIM, col_map=qkvo_block),
        _pack_rows_t(wt, cols=2 * e, row0=o_ga),
        _pack_rows(wgc, col_block=2 * LANES), wgri, wgrf, bgc, bgr,
        w_dw[0].astype(F32), row(b_dw), row(ln_a_g), row(ln_a_b),
        _pack_rows(w_a_out), row(hn_g), _pack_rows(w_b_out), _pack_rows(w_out),
        row(ln1_g), row(ln1_b),
    )
    ffn_w = (_pack_rows(w_ff1), _pack_rows(w_ff2), row(ln2_g), row(ln2_b))

    Lp = MXU_DIM
    ncp = 1
    x1p, convp, cp, np_, mp = _mixer(
        x_prompt.reshape(bp * sp, e),
        jnp.zeros((bp, CONV_BUF, e), F32),
        jnp.zeros((bp, N_HEADS, HEAD_DIM, HEAD_DIM), F32),
        jnp.zeros((bp, N_HEADS, 1, HEAD_DIM), F32),
        jnp.zeros((bp, 1, LANES), F32),
        weights, ns=1, nc=ncp, L=Lp, nj=sp // (ncp * Lp))
    x1s, convs, cs, ns_, ms = _mixer(
        x_sample.reshape(bs * ss, e),
        cache_conv[0].astype(F32),
        state_C[0].astype(F32),
        state_n[0].astype(F32)[:, :, None, :],
        _pad_m(state_m[0]),
        weights, ns=4, nc=1, L=ss, nj=1)

    yp = _ffn(x1p, *ffn_w, tile=512).reshape(bp, sp, e)
    ys = _ffn(x1s, *ffn_w, tile=512).reshape(bs, ss, e)

    def st(conv, c, n, m):
        return conv[None], c[None], n[:, :, 0, :][None], m[:, 0, :N_HEADS][None]

    return (yp, ys) + st(convp, cp, np_, mp) + st(convs, cs, ns_, ms)
```

```python
import functools

import jax
import jax.numpy as jnp
from jax import lax
from jax.experimental import pallas as pl
from jax.experimental.pallas import tpu as pltpu

D_MODEL = 1024
N_HEADS = 4
HEAD_DIM = 256
CONV_WIDTH = 31
CONV_BUF = CONV_WIDTH - 1
DEPTH = 1
ALPHA = (2.0 * DEPTH) ** 0.25
LN_EPS = 1e-5

LANES = 128
SUBLANES = 8
MXU_DIM = 256
HIST_ROWS = 32
CONV_ROW_BLOCK = 256
MASKED = -1e30
FFN_PART_ROWS = 256
FFN_TILE = 1024
VMEM_LIMIT_BYTES = 58 * 1024 * 1024

F32 = jnp.float32
BF16 = jnp.bfloat16


def _mm(a, b):
    return jnp.dot(a, b, preferred_element_type=F32)


def _mm_nt(a, b):
    return lax.dot_general(a, b, (((1,), (1,)), ((), ())), preferred_element_type=F32)


def _mm_tn(a, b):
    return lax.dot_general(a, b, (((0,), (0,)), ((), ())), preferred_element_type=F32)


def _bf16_terms(x):
    hi = x.astype(BF16)
    r1 = x - hi.astype(F32)
    mid = r1.astype(BF16)
    lo = (r1 - mid.astype(F32)).astype(BF16)
    return hi, mid, lo


def _ones_mm(ones, x):
    ob = ones.astype(BF16)
    hi, mid, lo = _bf16_terms(x)
    return _mm(ob, hi) + _mm(ob, mid) + _mm(ob, lo)


def _mm_ones(x, ones):
    ob = ones.astype(BF16)
    hi, mid, lo = _bf16_terms(x)
    return _mm(hi, ob) + _mm(mid, ob) + _mm(lo, ob)


PACK_ROWS = 1024
PACK_COLS = 512


def _pack_kernel(w_ref, o_ref):
    o_ref[...] = pltpu.bitcast(w_ref[...].astype(BF16), jnp.uint32)


def _pack_rows(w, cols=None, col_block=PACK_COLS, col_map=lambda j: j):
    if w.ndim == 2:
        w = w[None]
    _, k, n = w.shape
    cols = n if cols is None else cols
    kb = min(PACK_ROWS, k)
    return pl.pallas_call(
        _pack_kernel,
        grid=(k // kb, cols // col_block),
        in_specs=[pl.BlockSpec((None, kb, col_block), lambda i, j: (0, i, col_map(j)))],
        out_specs=pl.BlockSpec((kb // 2, col_block), lambda i, j: (i, j)),
        out_shape=jax.ShapeDtypeStruct((k // 2, cols), jnp.uint32),
        compiler_params=pltpu.CompilerParams(dimension_semantics=("arbitrary", "arbitrary")),
        name="pack_weight",
    )(w)


def _pack_t_kernel(wt_ref, o_ref):
    o_ref[...] = pltpu.bitcast(wt_ref[...].T.astype(BF16), jnp.uint32)


def _pack_rows_t(wt, cols=None, col_block=PACK_COLS, col_map=lambda j: j, row0=0):
    n, k = wt.shape
    cols = n if cols is None else cols
    if row0:
        in_spec = pl.BlockSpec((pl.Element(col_block), pl.Element(k)),
                               lambda j: (pl.multiple_of(row0 + col_map(j) * col_block, SUBLANES), 0))
    else:
        in_spec = pl.BlockSpec((col_block, k), lambda j: (col_map(j), 0))
    return pl.pallas_call(
        _pack_t_kernel,
        grid=(cols // col_block,),
        in_specs=[in_spec],
        out_specs=pl.BlockSpec((k // 2, col_block), lambda j: (0, j)),
        out_shape=jax.ShapeDtypeStruct((k // 2, cols), jnp.uint32),
        compiler_params=pltpu.CompilerParams(dimension_semantics=("arbitrary",)),
        name="pack_weight_t",
    )(wt)


def _w(ref, cols=slice(None)):
    return pltpu.bitcast(ref[:, cols], BF16)


def _sigmoid(x):
    return 0.5 * (jnp.tanh(0.5 * x) + 1.0)


def _log_sigmoid(x):
    return jnp.minimum(x, 0.0) - jnp.log(1.0 + jnp.exp(-jnp.abs(x)))


def _layer_norm(x, g, b):
    mu = jnp.mean(x, axis=-1, keepdims=True)
    xc = x - mu
    var = jnp.mean(xc * xc, axis=-1, keepdims=True)
    return xc * lax.rsqrt(var + LN_EPS) * g + b


def _mixer_kernel(ns, nc, L, nj,
                  x_ref, cache_ref, c_in, n_in, m_in,
                  wconv, wqkvo, wg, wgc, wgri, wgrf, bgc, bgr, wdw, bdw, lnag, lnab,
                  waout, hng, wbout, wout, ln1g, ln1b,
                  x1_ref, conv_out, c_out, n_out, m_out,
                  xb_scr, cb0, cb1, cb2, cb3, dw_scr, yain_scr, zq_scr, zg_scr, ybuf):
    j = pl.program_id(1)
    carried = nj > 1
    Lc = nc * L
    T = ns * Lc
    lo = HIST_ROWS - CONV_BUF
    quarter = D_MODEL // 4

    n_blk = D_MODEL // LANES
    per_q = quarter // LANES
    windows = (cb0, cb1, cb2, cb3)

    def window(b):
        return windows[b // per_q]

    def load_history():
        for s in range(ns):
            for b in range(n_blk):
                window(b)[s, b % per_q, lo:HIST_ROWS, :] = cache_ref[s, :, b * LANES:(b + 1) * LANES]

    if carried:
        @pl.when(j == 0)
        def _():
            c_out[...] = c_in[...]
            n_out[...] = n_in[...]
            m_out[...] = m_in[...]
            load_history()
    else:
        load_history()

    def zero_tile(tile):
        bits = pltpu.bitcast(tile, jnp.uint32)
        zero = lax.shift_right_logical(lax.shift_right_logical(bits, jnp.uint32(16)), jnp.uint32(16))
        return pltpu.bitcast(zero, F32)

    def zero_after(tile):
        return zero_tile(tile)[0:1]

    xb_scr[...] = x_ref[...].astype(BF16)

    def glu_quarter(i):
        cs = slice(i * quarter, (i + 1) * quarter)
        gs = slice(D_MODEL + i * quarter, D_MODEL + (i + 1) * quarter)
        xb = xb_scr[...]
        glu = _mm(xb, _w(wconv, cs)) * _sigmoid(_mm(xb, _w(wconv, gs)))
        for s in range(ns):
            for b in range(quarter // LANES):
                windows[i][s, b, HIST_ROWS:HIST_ROWS + Lc, :] = (
                    glu[s * Lc:(s + 1) * Lc, b * LANES:(b + 1) * LANES])
        return windows[i][ns - 1, per_q - 1,
                       HIST_ROWS + Lc - SUBLANES:HIST_ROWS + Lc, :]

    rb = min(CONV_ROW_BLOCK, Lc)

    def conv_block(i, after=None):
        cs = slice(i * LANES, (i + 1) * LANES)
        bias = bdw[:, cs] if after is None else bdw[:, cs] + zero_after(after)
        group = max(1, min(ns, CONV_ROW_BLOCK // rb))
        for s0 in range(0, ns, group):
            for r0 in range(0, Lc, rb):
                accs = [jnp.broadcast_to(bias, (rb, LANES))] * group
                for t in range(CONV_WIDTH):
                    wt = jnp.broadcast_to(wdw[t:t + 1, cs], (SUBLANES, LANES))
                    if t > 0:
                        wt = wt + zero_tile(accs[0][0:SUBLANES])
                    for g in range(group):
                        win = window(i)[s0 + g, i % per_q, lo + r0 + t:lo + r0 + t + rb, :]
                        accs[g] = accs[g] + (win.reshape(rb // SUBLANES, SUBLANES, LANES) * wt
                                             ).reshape(rb, LANES)
                for g in range(group):
                    dw_scr[(s0 + g) * Lc + r0:(s0 + g) * Lc + r0 + rb, cs] = accs[g]

    def conv_tail():
        for s in range(ns):
            for b in range(n_blk):
                tail = window(b)[s, b % per_q, HIST_ROWS + Lc - CONV_BUF:HIST_ROWS + Lc, :]
                conv_out[s, :, b * LANES:(b + 1) * LANES] = tail
                if carried:
                    window(b)[s, b % per_q, lo:HIST_ROWS, :] = tail

    def ln_swish(r0, r1, after):
        y = _layer_norm(dw_scr[r0:r1, :], lnag[...], lnab[...] + zero_after(after)[:, 0:1])
        yain_scr[r0:r1, :] = (y * _sigmoid(y)).astype(BF16)

    def gate_activations():
        xb = xb_scr[...]
        zgc = _mm(xb, _w(wgc)) + bgc[...]
        ig_col = zgc[:, :LANES]
        lf_col = _log_sigmoid(zgc[:, LANES:])
        ig_row = _mm_nt(wgri[...], xb) + bgr[:, 0:1]
        lf_row = _log_sigmoid(_mm_nt(wgrf[...], xb) + bgr[:, 1:2])
        return ig_col, lf_col, ig_row, lf_row

    def gate_cumsums(ig_col, lf_col, ig_row, lf_row):
        rows = lax.broadcasted_iota(jnp.int32, (L, L), 0)
        cols = lax.broadcasted_iota(jnp.int32, (L, L), 1)
        causal = rows >= cols
        tri_l = causal.astype(F32)
        tri_u = (rows <= cols).astype(F32)
        per_chunk = []
        for ci in range(ns * nc):
            sl = slice(ci * L, (ci + 1) * L)
            b_col = _ones_mm(tri_l, lf_col[sl])
            b_row = _mm_ones(lf_row[:, sl], tri_u)
            per_chunk.append((ig_col[sl], ig_row[:, sl], b_col, b_row))
        return causal, per_chunk

    def qkvo_head(h):
        hs = slice(h * D_MODEL, (h + 1) * D_MODEL)
        zq_scr[:, hs] = _mm(xb_scr[...], _w(wqkvo, hs))
        return zq_scr[T - SUBLANES:T, (h + 1) * D_MODEL - LANES:(h + 1) * D_MODEL]

    lane = lax.broadcasted_iota(jnp.int32, (1, LANES), 1)
    scale = HEAD_DIM ** -0.5

    def mlstm_head(s, c, h, causal, chunk_gates, after):
        ig_col, ig_row, b_col, b_row = chunk_gates
        sl = slice((s * nc + c) * L, (s * nc + c + 1) * L)
        first = c == 0
        c_prev, n_prev, m_prev = ((c_out, n_out, m_out) if carried or not first
                                  else (c_in, n_in, m_in))
        b_c = b_col[:, h:h + 1]
        if ns == 1:
            b_c = b_c + zero_after(after)[:, 0:1]
        ig_c = ig_col[:, h:h + 1]
        b_r = b_row[h:h + 1, :]
        ig_r = ig_row[h:h + 1, :]
        m0 = m_prev[s][:, h:h + 1]
        a_c = b_c + m0
        dm = jnp.where(causal, b_c + (ig_r - b_r), MASKED)
        m_c = jnp.maximum(a_c, jnp.max(dm, axis=1, keepdims=True))
        w_intra = jnp.exp(dm - m_c)
        w_inter = jnp.exp(a_c - m_c)

        base = h * D_MODEL
        q = zq_scr[sl, base:base + HEAD_DIM]
        k = zq_scr[sl, base + HEAD_DIM:base + 2 * HEAD_DIM] * scale
        v = zq_scr[sl, base + 2 * HEAD_DIM:base + 3 * HEAD_DIM]
        o = zq_scr[sl, base + 3 * HEAD_DIM:base + 4 * HEAD_DIM]
        qb = q.astype(BF16)
        vb = v.astype(BF16)
        c0 = c_prev[s, h]
        n0 = n_prev[s, h]

        smat = _mm_nt(qb, k.astype(BF16)) * w_intra
        num = _mm(smat.astype(BF16), vb) + w_inter * _mm(qb, c0.astype(BF16))
        den = (jnp.sum(smat, axis=1, keepdims=True)
               + w_inter * jnp.sum(q * n0, axis=1, keepdims=True))
        hh = num / jnp.maximum(jnp.abs(den), jnp.exp(-m_c))

        m_end = m_c[L - 1:L]
        w_end = jnp.exp(b_c[L - 1:L] - b_c + ig_c - m_end)
        decay = jnp.exp(a_c[L - 1:L] - m_end)
        kw = k * w_end
        c_out[s, h] = decay * c0 + _mm_tn(kw.astype(BF16), vb)
        n_out[s, h] = decay * n0 + jnp.sum(kw, axis=0, keepdims=True)
        m_out[s] = jnp.where(lane == h, m_end, m_out[s] if (carried or not first or h > 0)
                             else m_in[s])

        mu = jnp.mean(hh, axis=1, keepdims=True)
        hc = hh - mu
        var = jnp.mean(hc * hc, axis=1, keepdims=True)
        hn = hc * lax.rsqrt(var + LN_EPS) * hng[:, h * HEAD_DIM:(h + 1) * HEAD_DIM]
        ybuf[sl, h * HEAD_DIM:(h + 1) * HEAD_DIM] = (_sigmoid(o) * hn).astype(BF16)

    def merge_gates_quarter(i):
        cs = slice(i * quarter, (i + 1) * quarter)
        gs = slice(D_MODEL + i * quarter, D_MODEL + (i + 1) * quarter)
        xb = xb_scr[...]
        zg_scr[:, cs] = _sigmoid(_mm(xb, _w(wg, cs)))
        zg_scr[:, gs] = _sigmoid(_mm(xb, _w(wg, gs)))
        return zg_scr[T - SUBLANES:T, D_MODEL + (i + 1) * quarter - LANES:D_MODEL + (i + 1) * quarter]

    a0 = glu_quarter(0)
    conv_block(0)
    conv_block(1)
    gate_act = gate_activations()
    a1 = glu_quarter(1)
    conv_block(2)
    conv_block(3)
    causal, chunk_gates = gate_cumsums(*gate_act)
    a2 = glu_quarter(2)
    conv_block(4)
    a3 = glu_quarter(3)
    conv_block(5, after=a1)
    b0 = qkvo_head(0)
    conv_block(6, after=a2)
    b1 = qkvo_head(1)
    conv_block(7, after=b0)
    conv_tail()
    b2 = qkvo_head(2)

    def heads(h, after):
        for s in range(ns):
            for c in range(nc):
                mlstm_head(s, c, h, causal, chunk_gates[s * nc + c], after)

    half = T // 2
    heads(0, b1)
    ln_swish(0, half, b2)
    b3 = qkvo_head(3)
    heads(1, b2)
    ln_swish(half, T, b3)
    merge_gates_quarter(0)
    c1 = merge_gates_quarter(1)
    heads(2, b3)
    merge_gates_quarter(2)
    merge_gates_quarter(3)
    ya_gated = zg_scr[:, :D_MODEL] * _mm(yain_scr[...], _w(waout))
    heads(3, c1)

    y_b = _mm(ybuf[...], _w(wbout))
    merged = (ya_gated + zg_scr[:, D_MODEL:] * y_b).astype(BF16)
    res = ALPHA * x_ref[...] + _mm(merged, _w(wout))
    x1_ref[...] = _layer_norm(res, ln1g[...], ln1b[...])


def _ffn_kernel(x1_ref, wff1, wff2, ln2g, ln2b, out_ref):
    rows = FFN_PART_ROWS
    parts = x1_ref.shape[0] // rows
    hid = []
    for r in range(parts):
        x1 = x1_ref[r * rows:(r + 1) * rows, :]
        h = jnp.maximum(_mm(x1.astype(BF16), _w(wff1)), 0.0)
        hid.append((h * h).astype(BF16))
    for r in range(parts):
        x1 = x1_ref[r * rows:(r + 1) * rows, :]
        ff = _mm(hid[r], _w(wff2))
        out_ref[r * rows:(r + 1) * rows, :] = _layer_norm(ALPHA * x1 + ff, ln2g[...], ln2b[...])


def _resident(shape):
    nd = len(shape)
    return pl.BlockSpec(shape, lambda *_: (0,) * nd, pipeline_mode=pl.Buffered(1))


def _mixer(x2d, cache, c0, n0, m0, weights, *, ns, nc, L, nj):
    Lc = nc * L
    T = ns * Lc
    n_tok = x2d.shape[0]
    G = n_tok // (T * nj)
    n_seq = G * ns
    state_specs = [
        pl.BlockSpec((ns, CONV_BUF, D_MODEL), lambda g, j: (g, 0, 0)),
        pl.BlockSpec((ns, N_HEADS, HEAD_DIM, HEAD_DIM), lambda g, j: (g, 0, 0, 0)),
        pl.BlockSpec((ns, N_HEADS, 1, HEAD_DIM), lambda g, j: (g, 0, 0, 0)),
        pl.BlockSpec((ns, 1, LANES), lambda g, j: (g, 0, 0)),
    ]
    state_in_specs = state_specs
    x_spec = pl.BlockSpec((T, D_MODEL), lambda g, j: (g * nj + j, 0))
    return pl.pallas_call(
        functools.partial(_mixer_kernel, ns, nc, L, nj),
        grid=(G, nj),
        in_specs=[x_spec] + state_in_specs + [_resident(w.shape) for w in weights],
        out_specs=[x_spec] + state_specs,
        out_shape=[
            jax.ShapeDtypeStruct((n_tok, D_MODEL), F32),
            jax.ShapeDtypeStruct((n_seq, CONV_BUF, D_MODEL), F32),
            jax.ShapeDtypeStruct((n_seq, N_HEADS, HEAD_DIM, HEAD_DIM), F32),
            jax.ShapeDtypeStruct((n_seq, N_HEADS, 1, HEAD_DIM), F32),
            jax.ShapeDtypeStruct((n_seq, 1, LANES), F32),
        ],
        scratch_shapes=[
            pltpu.VMEM((T, D_MODEL), BF16),
            *[pltpu.VMEM((ns, D_MODEL // 4 // LANES, HIST_ROWS + Lc, LANES), F32)
              for _ in range(4)],
            pltpu.VMEM((T, D_MODEL), F32),
            pltpu.VMEM((T, D_MODEL), BF16),
            pltpu.VMEM((T, 4 * D_MODEL), F32),
            pltpu.VMEM((T, 2 * D_MODEL), F32),
            pltpu.VMEM((T, D_MODEL), BF16),
        ],
        compiler_params=pltpu.CompilerParams(
            dimension_semantics=("arbitrary", "arbitrary"),
            vmem_limit_bytes=VMEM_LIMIT_BYTES),
        name=f"mixer_ns{ns}",
    )(x2d, cache, c0, n0, m0, *weights)


def _ffn(x1, wff1, wff2, ln2g, ln2b, *, tile):
    n_tok = x1.shape[0]
    x_spec = pl.BlockSpec((tile, D_MODEL), lambda i: (i, 0))
    weights = (wff1, wff2, ln2g, ln2b)
    return pl.pallas_call(
        _ffn_kernel,
        grid=(n_tok // tile,),
        in_specs=[x_spec] + [_resident(w.shape) for w in weights],
        out_specs=x_spec,
        out_shape=jax.ShapeDtypeStruct((n_tok, D_MODEL), F32),
        compiler_params=pltpu.CompilerParams(
            dimension_semantics=("arbitrary",),
            vmem_limit_bytes=VMEM_LIMIT_BYTES),
        name="ffn",
    )(x1, *weights)


def _pad_m(m):
    return jnp.pad(m.astype(F32), ((0, 0), (0, LANES - N_HEADS)))[:, None, :]


def kernel(x_prompt, x_sample, cache_conv, state_C, state_n, state_m, w_in, b_gate, w_dw, b_dw,
           ln_a_g, ln_a_b, w_a_out, hn_g, w_b_out, w_out, ln1_g, ln1_b, w_ff1, w_ff2, ln2_g, ln2_b):
    assert w_in.shape[0] == DEPTH == 1
    bp, sp, _ = x_prompt.shape
    bs, ss, _ = x_sample.shape
    e = D_MODEL
    o_qkvo = 2 * e
    o_gate = o_qkvo + 4 * e
    o_ga = o_gate + 2 * N_HEADS

    def row(v):
        return v[0].astype(F32)[None, :]

    qkvo_blk0 = o_qkvo // HEAD_DIM

    def qkvo_block(j):
        return qkvo_blk0 + (j % 4) * N_HEADS + j // 4

    wt = jnp.swapaxes(w_in, 1, 2)[0]
    w_i_t = wt[o_gate:o_gate + N_HEADS]
    w_f_t = wt[o_gate + N_HEADS:o_ga]
    zpad = jnp.zeros((e, LANES - N_HEADS), wt.dtype)
    wgc = jnp.concatenate([w_i_t.T, zpad, w_f_t.T, zpad], axis=1)
    rpad = jnp.zeros((SUBLANES - N_HEADS, e), wt.dtype)
    wgri = jnp.concatenate([w_i_t, rpad], axis=0).astype(BF16)
    wgrf = jnp.concatenate([w_f_t, rpad], axis=0).astype(BF16)
    bg = b_gate[0].astype(F32)
    bpad = jnp.zeros((LANES - N_HEADS,), F32)
    bgc = jnp.concatenate([bg[:N_HEADS], bpad, bg[N_HEADS:], bpad])[None, :]
    bgr = jnp.pad(bg.reshape(2, N_HEADS).T, ((0, SUBLANES - N_HEADS), (0, 0)))

    weights = (
        _pack_rows_t(wt, cols=o_qkvo),
        _pack_rows_t(wt, cols=4 * e, col_block=HEAD_DIM, col_map=qkvo_block),
        _pack_rows_t(wt, cols=2 * e, row0=o_ga),
        _pack_rows(wgc, col_block=2 * LANES), wgri, wgrf, bgc, bgr,
        w_dw[0].astype(F32), row(b_dw), row(ln_a_g), row(ln_a_b),
        _pack_rows(w_a_out), row(hn_g), _pack_rows(w_b_out), _pack_rows(w_out),
        row(ln1_g), row(ln1_b),
    )
    ffn_w = (_pack_rows(w_ff1), _pack_rows(w_ff2), row(ln2_g), row(ln2_b))

    Lp = MXU_DIM
    ncp = 1
    x1p, convp, cp, np_, mp = _mixer(
        x_prompt.reshape(bp * sp, e),
        jnp.zeros((bp, CONV_BUF, e), F32),
        jnp.zeros((bp, N_HEADS, HEAD_DIM, HEAD_DIM), F32),
        jnp.zeros((bp, N_HEADS, 1, HEAD_DIM), F32),
        jnp.zeros((bp, 1, LANES), F32),
        weights, ns=1, nc=ncp, L=Lp, nj=sp // (ncp * Lp))
    x1s, convs, cs, ns_, ms = _mixer(
        x_sample.reshape(bs * ss, e),
        cache_conv[0].astype(F32),
        state_C[0].astype(F32),
        state_n[0].astype(F32)[:, :, None, :],
        _pad_m(state_m[0]),
        weights, ns=4, nc=1, L=ss, nj=1)

    yp = _ffn(x1p, *ffn_w, tile=FFN_TILE).reshape(bp, sp, e)
    ys = _ffn(x1s, *ffn_w, tile=FFN_TILE).reshape(bs, ss, e)

    def st(conv, c, n, m):
        return conv[None], c[None], n[:, :, 0, :][None], m[:, 0, :N_HEADS][None]

    return (yp, ys) + st(convp, cp, np_, mp) + st(convs, cs, ns_, ms)
```

```python
import functools

import jax
import jax.numpy as jnp
from jax import lax
from jax.experimental import pallas as pl
from jax.experimental.pallas import tpu as pltpu

D_MODEL = 1024
N_HEADS = 4
HEAD_DIM = 256
CONV_WIDTH = 31
CONV_BUF = CONV_WIDTH - 1
D_FF = 4 * D_MODEL
DEPTH = 1
ALPHA = (2.0 * DEPTH) ** 0.25
LN_EPS = 1e-5

LANES = 128
SUBLANES = 8
MXU_DIM = 256
HIST_ROWS = 32
CONV_ROW_BLOCK = 256
MASKED = -1e30
VMEM_LIMIT_BYTES = 58 * 1024 * 1024

F32 = jnp.float32
BF16 = jnp.bfloat16


def _mm(a, b):
    return jnp.dot(a, b, preferred_element_type=F32)


def _mm_nt(a, b):
    return lax.dot_general(a, b, (((1,), (1,)), ((), ())), preferred_element_type=F32)


def _mm_tn(a, b):
    return lax.dot_general(a, b, (((0,), (0,)), ((), ())), preferred_element_type=F32)


def _bf16_terms(x):
    hi = x.astype(BF16)
    r1 = x - hi.astype(F32)
    mid = r1.astype(BF16)
    lo = (r1 - mid.astype(F32)).astype(BF16)
    return hi, mid, lo


def _ones_mm(ones, x):
    ob = ones.astype(BF16)
    hi, mid, lo = _bf16_terms(x)
    return _mm(ob, hi) + _mm(ob, mid) + _mm(ob, lo)


def _mm_ones(x, ones):
    ob = ones.astype(BF16)
    hi, mid, lo = _bf16_terms(x)
    return _mm(hi, ob) + _mm(mid, ob) + _mm(lo, ob)


PACK_ROWS = 1024
PACK_COLS = 1024


def _pack_kernel(w_ref, o_ref):
    o_ref[...] = pltpu.bitcast(w_ref[...].astype(BF16), jnp.uint32)


def _pack_rows(w, cols=None, col_block=PACK_COLS, col_map=lambda j: j):
    if w.ndim == 2:
        w = w[None]
    _, k, n = w.shape
    cols = n if cols is None else cols
    kb = min(PACK_ROWS, k)
    return pl.pallas_call(
        _pack_kernel,
        grid=(k // kb, cols // col_block),
        in_specs=[pl.BlockSpec((None, kb, col_block), lambda i, j: (0, i, col_map(j)))],
        out_specs=pl.BlockSpec((kb // 2, col_block), lambda i, j: (i, j)),
        out_shape=jax.ShapeDtypeStruct((k // 2, cols), jnp.uint32),
        compiler_params=pltpu.CompilerParams(dimension_semantics=("arbitrary", "arbitrary")),
        name="pack_weight",
    )(w)


def _pack_t_kernel(*refs):
    *wt_refs, o_ref = refs
    width = o_ref.shape[1] // len(wt_refs)
    for i, wt_ref in enumerate(wt_refs):
        o_ref[:, i * width:(i + 1) * width] = pltpu.bitcast(wt_ref[...].T.astype(BF16), jnp.uint32)


def _pack_rows_t(wt, cols=None, col_block=PACK_COLS, col_map=lambda j: j, row0=0, per_step=1):
    n, k = wt.shape
    cols = n if cols is None else cols

    def in_spec(p):
        if row0:
            return pl.BlockSpec(
                (pl.Element(col_block), pl.Element(k)),
                lambda j: (pl.multiple_of(row0 + col_map(per_step * j + p) * col_block, SUBLANES), 0))
        return pl.BlockSpec((col_block, k), lambda j: (col_map(per_step * j + p), 0))

    return pl.pallas_call(
        _pack_t_kernel,
        grid=(cols // (col_block * per_step),),
        in_specs=[in_spec(p) for p in range(per_step)],
        out_specs=pl.BlockSpec((k // 2, col_block * per_step), lambda j: (0, j)),
        out_shape=jax.ShapeDtypeStruct((k // 2, cols), jnp.uint32),
        compiler_params=pltpu.CompilerParams(dimension_semantics=("arbitrary",)),
        name="pack_weight_t",
    )(*([wt] * per_step))


def _w(ref, cols=slice(None)):
    return pltpu.bitcast(ref[:, cols], BF16)


def _sigmoid(x):
    return 0.5 * (jnp.tanh(0.5 * x) + 1.0)


def _log_sigmoid(x):
    return jnp.minimum(x, 0.0) - jnp.log(1.0 + jnp.exp(-jnp.abs(x)))


def _layer_norm(x, g, b):
    mu = jnp.mean(x, axis=-1, keepdims=True)
    xc = x - mu
    var = jnp.mean(xc * xc, axis=-1, keepdims=True)
    return xc * lax.rsqrt(var + LN_EPS) * g + b


def _mixer_kernel(ns, nc, L, nj,
                  x_ref, cache_ref, c_in, n_in, m_in,
                  wconv, wqkvo, wg, wgc, wgri, wgrf, bgc, bgr, wdw, bdw, lnag, lnab,
                  waout, hng, wbout, wout, ln1g, ln1b,
                  x1_ref, conv_out, c_out, n_out, m_out,
                  xb_scr, cb0, cb1, cb2, cb3, dw_scr, yain_scr, zq_scr, zg_scr, ybuf):
    j = pl.program_id(1)
    carried = nj > 1
    Lc = nc * L
    T = ns * Lc
    lo = HIST_ROWS - CONV_BUF
    quarter = D_MODEL // 4

    n_blk = D_MODEL // LANES
    per_q = quarter // LANES
    windows = (cb0, cb1, cb2, cb3)

    def window(b):
        return windows[b // per_q]

    def load_history():
        for s in range(ns):
            for b in range(n_blk):
                window(b)[s, b % per_q, lo:HIST_ROWS, :] = cache_ref[s, :, b * LANES:(b + 1) * LANES]

    if carried:
        @pl.when(j == 0)
        def _():
            c_out[...] = c_in[...]
            n_out[...] = n_in[...]
            m_out[...] = m_in[...]
            load_history()
    else:
        load_history()

    def zero_tile(tile):
        bits = pltpu.bitcast(tile, jnp.uint32)
        zero = lax.shift_right_logical(lax.shift_right_logical(bits, jnp.uint32(16)), jnp.uint32(16))
        return pltpu.bitcast(zero, F32)

    def zero_after(tile):
        return zero_tile(tile)[0:1]

    xb_scr[...] = x_ref[...].astype(BF16)

    def glu_quarter(i):
        cs = slice(i * quarter, (i + 1) * quarter)
        gs = slice(D_MODEL + i * quarter, D_MODEL + (i + 1) * quarter)
        xb = xb_scr[...]
        glu = _mm(xb, _w(wconv, cs)) * _sigmoid(_mm(xb, _w(wconv, gs)))
        for s in range(ns):
            for b in range(quarter // LANES):
                windows[i][s, b, HIST_ROWS:HIST_ROWS + Lc, :] = (
                    glu[s * Lc:(s + 1) * Lc, b * LANES:(b + 1) * LANES])
        return windows[i][ns - 1, per_q - 1,
                       HIST_ROWS + Lc - SUBLANES:HIST_ROWS + Lc, :]

    rb = min(CONV_ROW_BLOCK, Lc)

    def conv_block(i, after=None):
        cs = slice(i * LANES, (i + 1) * LANES)
        bias = bdw[:, cs] if after is None else bdw[:, cs] + zero_after(after)
        group = max(1, min(ns, CONV_ROW_BLOCK // rb))
        for s0 in range(0, ns, group):
            for r0 in range(0, Lc, rb):
                accs = [jnp.broadcast_to(bias, (rb, LANES))] * group
                for t in range(CONV_WIDTH):
                    wt = jnp.broadcast_to(wdw[t:t + 1, cs], (SUBLANES, LANES))
                    if t > 0:
                        wt = wt + zero_tile(accs[0][0:SUBLANES])
                    for g in range(group):
                        win = window(i)[s0 + g, i % per_q, lo + r0 + t:lo + r0 + t + rb, :]
                        accs[g] = accs[g] + (win.reshape(rb // SUBLANES, SUBLANES, LANES) * wt
                                             ).reshape(rb, LANES)
                for g in range(group):
                    dw_scr[(s0 + g) * Lc + r0:(s0 + g) * Lc + r0 + rb, cs] = accs[g]

    def conv_tail():
        for s in range(ns):
            for b in range(n_blk):
                tail = window(b)[s, b % per_q, HIST_ROWS + Lc - CONV_BUF:HIST_ROWS + Lc, :]
                conv_out[s, :, b * LANES:(b + 1) * LANES] = tail
                if carried:
                    window(b)[s, b % per_q, lo:HIST_ROWS, :] = tail

    def ln_swish(r0, r1, after):
        y = _layer_norm(dw_scr[r0:r1, :], lnag[...], lnab[...] + zero_after(after)[:, 0:1])
        yain_scr[r0:r1, :] = (y * _sigmoid(y)).astype(BF16)

    def gate_activations():
        xb = xb_scr[...]
        zgc = _mm(xb, _w(wgc)) + bgc[...]
        ig_col = zgc[:, :LANES]
        lf_col = _log_sigmoid(zgc[:, LANES:])
        ig_row = _mm_nt(wgri[...], xb) + bgr[:, 0:1]
        lf_row = _log_sigmoid(_mm_nt(wgrf[...], xb) + bgr[:, 1:2])
        return ig_col, lf_col, ig_row, lf_row

    def gate_cumsums(ig_col, lf_col, ig_row, lf_row):
        rows = lax.broadcasted_iota(jnp.int32, (L, L), 0)
        cols = lax.broadcasted_iota(jnp.int32, (L, L), 1)
        causal = rows >= cols
        tri_l = causal.astype(F32)
        tri_u = (rows <= cols).astype(F32)
        per_chunk = []
        for ci in range(ns * nc):
            sl = slice(ci * L, (ci + 1) * L)
            b_col = _ones_mm(tri_l, lf_col[sl])
            b_row = _mm_ones(lf_row[:, sl], tri_u)
            per_chunk.append((ig_col[sl], ig_row[:, sl], b_col, b_row))
        return causal, per_chunk

    def qkvo_head(h):
        hs = slice(h * D_MODEL, (h + 1) * D_MODEL)
        zq_scr[:, hs] = _mm(xb_scr[...], _w(wqkvo, hs))
        return zq_scr[T - SUBLANES:T, (h + 1) * D_MODEL - LANES:(h + 1) * D_MODEL]

    lane = lax.broadcasted_iota(jnp.int32, (1, LANES), 1)
    scale = HEAD_DIM ** -0.5

    def mlstm_head(s, c, h, causal, chunk_gates, after):
        ig_col, ig_row, b_col, b_row = chunk_gates
        sl = slice((s * nc + c) * L, (s * nc + c + 1) * L)
        first = c == 0
        c_prev, n_prev, m_prev = ((c_out, n_out, m_out) if carried or not first
                                  else (c_in, n_in, m_in))
        b_c = b_col[:, h:h + 1]
        if ns == 1:
            b_c = b_c + zero_after(after)[:, 0:1]
        ig_c = ig_col[:, h:h + 1]
        b_r = b_row[h:h + 1, :]
        ig_r = ig_row[h:h + 1, :]
        m0 = m_prev[s][:, h:h + 1]
        a_c = b_c + m0
        dm = jnp.where(causal, b_c + (ig_r - b_r), MASKED)
        m_c = jnp.maximum(a_c, jnp.max(dm, axis=1, keepdims=True))
        w_intra = jnp.exp(dm - m_c)
        w_inter = jnp.exp(a_c - m_c)

        base = h * D_MODEL
        q = zq_scr[sl, base:base + HEAD_DIM]
        k = zq_scr[sl, base + HEAD_DIM:base + 2 * HEAD_DIM] * scale
        v = zq_scr[sl, base + 2 * HEAD_DIM:base + 3 * HEAD_DIM]
        o = zq_scr[sl, base + 3 * HEAD_DIM:base + 4 * HEAD_DIM]
        qb = q.astype(BF16)
        vb = v.astype(BF16)
        c0 = c_prev[s, h]
        n0 = n_prev[s, h]

        smat = _mm_nt(qb, k.astype(BF16)) * w_intra
        num = _mm(smat.astype(BF16), vb) + w_inter * _mm(qb, c0.astype(BF16))
        den = (jnp.sum(smat, axis=1, keepdims=True)
               + w_inter * jnp.sum(q * n0, axis=1, keepdims=True))
        hh = num / jnp.maximum(jnp.abs(den), jnp.exp(-m_c))

        m_end = m_c[L - 1:L]
        w_end = jnp.exp(b_c[L - 1:L] - b_c + ig_c - m_end)
        decay = jnp.exp(a_c[L - 1:L] - m_end)
        kw = k * w_end
        c_out[s, h] = decay * c0 + _mm_tn(kw.astype(BF16), vb)
        n_out[s, h] = decay * n0 + jnp.sum(kw, axis=0, keepdims=True)
        m_out[s] = jnp.where(lane == h, m_end, m_out[s] if (carried or not first or h > 0)
                             else m_in[s])

        mu = jnp.mean(hh, axis=1, keepdims=True)
        hc = hh - mu
        var = jnp.mean(hc * hc, axis=1, keepdims=True)
        hn = hc * lax.rsqrt(var + LN_EPS) * hng[:, h * HEAD_DIM:(h + 1) * HEAD_DIM]
        ybuf[sl, h * HEAD_DIM:(h + 1) * HEAD_DIM] = (_sigmoid(o) * hn).astype(BF16)

    def merge_gates_quarter(i):
        cs = slice(i * quarter, (i + 1) * quarter)
        gs = slice(D_MODEL + i * quarter, D_MODEL + (i + 1) * quarter)
        xb = xb_scr[...]
        zg_scr[:, cs] = _sigmoid(_mm(xb, _w(wg, cs)))
        zg_scr[:, gs] = _sigmoid(_mm(xb, _w(wg, gs)))
        return zg_scr[T - SUBLANES:T, D_MODEL + (i + 1) * quarter - LANES:D_MODEL + (i + 1) * quarter]

    a0 = glu_quarter(0)
    conv_block(0)
    conv_block(1)
    gate_act = gate_activations()
    a1 = glu_quarter(1)
    conv_block(2)
    conv_block(3)
    causal, chunk_gates = gate_cumsums(*gate_act)
    a2 = glu_quarter(2)
    conv_block(4)
    a3 = glu_quarter(3)
    conv_block(5, after=a1)
    b0 = qkvo_head(0)
    conv_block(6, after=a2)
    b1 = qkvo_head(1)
    conv_block(7, after=b0)
    conv_tail()
    b2 = qkvo_head(2)

    def heads(h, after):
        for s in range(ns):
            for c in range(nc):
                mlstm_head(s, c, h, causal, chunk_gates[s * nc + c], after)

    half = T // 2
    heads(0, b1)
    ln_swish(0, half, b2)
    b3 = qkvo_head(3)
    heads(1, b2)
    ln_swish(half, T, b3)
    merge_gates_quarter(0)
    c1 = merge_gates_quarter(1)
    heads(2, b3)
    merge_gates_quarter(2)
    merge_gates_quarter(3)
    ya_gated = zg_scr[:, :D_MODEL] * _mm(yain_scr[...], _w(waout))
    heads(3, c1)

    y_b = _mm(ybuf[...], _w(wbout))
    merged = (ya_gated + zg_scr[:, D_MODEL:] * y_b).astype(BF16)
    res = ALPHA * x_ref[...] + _mm(merged, _w(wout))
    x1_ref[...] = _layer_norm(res, ln1g[...], ln1b[...])


def _ffn_kernel(x1_ref, wff1, wff2, ln2g, ln2b, out_ref):
    rows = x1_ref.shape[0] // 2
    hid = []
    for r in range(2):
        x1 = x1_ref[r * rows:(r + 1) * rows, :]
        h = jnp.maximum(_mm(x1.astype(BF16), _w(wff1)), 0.0)
        hid.append((h * h).astype(BF16))
    for r in range(2):
        x1 = x1_ref[r * rows:(r + 1) * rows, :]
        ff = _mm(hid[r], _w(wff2))
        out_ref[r * rows:(r + 1) * rows, :] = _layer_norm(ALPHA * x1 + ff, ln2g[...], ln2b[...])


def _resident(shape):
    nd = len(shape)
    return pl.BlockSpec(shape, lambda *_: (0,) * nd, pipeline_mode=pl.Buffered(1))


def _mixer(x2d, cache, c0, n0, m0, weights, *, ns, nc, L, nj):
    Lc = nc * L
    T = ns * Lc
    n_tok = x2d.shape[0]
    G = n_tok // (T * nj)
    n_seq = G * ns
    state_specs = [
        pl.BlockSpec((ns, CONV_BUF, D_MODEL), lambda g, j: (g, 0, 0)),
        pl.BlockSpec((ns, N_HEADS, HEAD_DIM, HEAD_DIM), lambda g, j: (g, 0, 0, 0)),
        pl.BlockSpec((ns, N_HEADS, 1, HEAD_DIM), lambda g, j: (g, 0, 0, 0)),
        pl.BlockSpec((ns, 1, LANES), lambda g, j: (g, 0, 0)),
    ]
    state_in_specs = state_specs
    x_spec = pl.BlockSpec((T, D_MODEL), lambda g, j: (g * nj + j, 0))
    return pl.pallas_call(
        functools.partial(_mixer_kernel, ns, nc, L, nj),
        grid=(G, nj),
        in_specs=[x_spec] + state_in_specs + [_resident(w.shape) for w in weights],
        out_specs=[x_spec] + state_specs,
        out_shape=[
            jax.ShapeDtypeStruct((n_tok, D_MODEL), F32),
            jax.ShapeDtypeStruct((n_seq, CONV_BUF, D_MODEL), F32),
            jax.ShapeDtypeStruct((n_seq, N_HEADS, HEAD_DIM, HEAD_DIM), F32),
            jax.ShapeDtypeStruct((n_seq, N_HEADS, 1, HEAD_DIM), F32),
            jax.ShapeDtypeStruct((n_seq, 1, LANES), F32),
        ],
        scratch_shapes=[
            pltpu.VMEM((T, D_MODEL), BF16),
            *[pltpu.VMEM((ns, D_MODEL // 4 // LANES, HIST_ROWS + Lc, LANES), F32)
              for _ in range(4)],
            pltpu.VMEM((T, D_MODEL), F32),
            pltpu.VMEM((T, D_MODEL), BF16),
            pltpu.VMEM((T, 4 * D_MODEL), F32),
            pltpu.VMEM((T, 2 * D_MODEL), F32),
            pltpu.VMEM((T, D_MODEL), BF16),
        ],
        compiler_params=pltpu.CompilerParams(
            dimension_semantics=("arbitrary", "arbitrary"),
            vmem_limit_bytes=VMEM_LIMIT_BYTES),
        name=f"mixer_ns{ns}",
    )(x2d, cache, c0, n0, m0, *weights)


def _ffn(x1, wff1, wff2, ln2g, ln2b, *, tile):
    n_tok = x1.shape[0]
    x_spec = pl.BlockSpec((tile, D_MODEL), lambda i: (i, 0))
    weights = (wff1, wff2, ln2g, ln2b)
    return pl.pallas_call(
        _ffn_kernel,
        grid=(n_tok // tile,),
        in_specs=[x_spec] + [_resident(w.shape) for w in weights],
        out_specs=x_spec,
        out_shape=jax.ShapeDtypeStruct((n_tok, D_MODEL), F32),
        compiler_params=pltpu.CompilerParams(
            dimension_semantics=("arbitrary",),
            vmem_limit_bytes=VMEM_LIMIT_BYTES),
        name="ffn",
    )(x1, *weights)


def _pad_m(m):
    return jnp.pad(m.astype(F32), ((0, 0), (0, LANES - N_HEADS)))[:, None, :]


def kernel(x_prompt, x_sample, cache_conv, state_C, state_n, state_m, w_in, b_gate, w_dw, b_dw,
           ln_a_g, ln_a_b, w_a_out, hn_g, w_b_out, w_out, ln1_g, ln1_b, w_ff1, w_ff2, ln2_g, ln2_b):
    assert w_in.shape[0] == DEPTH == 1
    bp, sp, _ = x_prompt.shape
    bs, ss, _ = x_sample.shape
    e = D_MODEL
    o_qkvo = 2 * e
    o_gate = o_qkvo + 4 * e
    o_ga = o_gate + 2 * N_HEADS

    def row(v):
        return v[0].astype(F32)[None, :]

    qkvo_blk0 = o_qkvo // HEAD_DIM

    def qkvo_block(j):
        return qkvo_blk0 + (j % 4) * N_HEADS + j // 4

    wt = jnp.swapaxes(w_in, 1, 2)[0]
    w_i_t = wt[o_gate:o_gate + N_HEADS]
    w_f_t = wt[o_gate + N_HEADS:o_ga]
    zpad = jnp.zeros((e, LANES - N_HEADS), wt.dtype)
    wgc = jnp.concatenate([w_i_t.T, zpad, w_f_t.T, zpad], axis=1)
    rpad = jnp.zeros((SUBLANES - N_HEADS, e), wt.dtype)
    wgri = jnp.concatenate([w_i_t, rpad], axis=0).astype(BF16)
    wgrf = jnp.concatenate([w_f_t, rpad], axis=0).astype(BF16)
    bg = b_gate[0].astype(F32)
    bpad = jnp.zeros((LANES - N_HEADS,), F32)
    bgc = jnp.concatenate([bg[:N_HEADS], bpad, bg[N_HEADS:], bpad])[None, :]
    bgr = jnp.pad(bg.reshape(2, N_HEADS).T, ((0, SUBLANES - N_HEADS), (0, 0)))

    weights = (
        _pack_rows_t(wt, cols=o_qkvo),
        _pack_rows_t(wt, cols=4 * e, col_block=HEAD_DIM, col_map=qkvo_block, per_step=4),
        _pack_rows_t(wt, cols=2 * e, row0=o_ga),
        _pack_rows(wgc, col_block=2 * LANES), wgri, wgrf, bgc, bgr,
        w_dw[0].astype(F32), row(b_dw), row(ln_a_g), row(ln_a_b),
        _pack_rows(w_a_out), row(hn_g), _pack_rows(w_b_out), _pack_rows(w_out),
        row(ln1_g), row(ln1_b),
    )
    ffn_w = (_pack_rows(w_ff1), _pack_rows(w_ff2), row(ln2_g), row(ln2_b))

    Lp = MXU_DIM
    ncp = 1
    x1p, convp, cp, np_, mp = _mixer(
        x_prompt.reshape(bp * sp, e),
        jnp.zeros((bp, CONV_BUF, e), F32),
        jnp.zeros((bp, N_HEADS, HEAD_DIM, HEAD_DIM), F32),
        jnp.zeros((bp, N_HEADS, 1, HEAD_DIM), F32),
        jnp.zeros((bp, 1, LANES), F32),
        weights, ns=1, nc=ncp, L=Lp, nj=sp // (ncp * Lp))
    x1s, convs, cs, ns_, ms = _mixer(
        x_sample.reshape(bs * ss, e),
        cache_conv[0].astype(F32),
        state_C[0].astype(F32),
        state_n[0].astype(F32)[:, :, None, :],
        _pad_m(state_m[0]),
        weights, ns=4, nc=1, L=ss, nj=1)

    yp = _ffn(x1p, *ffn_w, tile=512).reshape(bp, sp, e)
    ys = _ffn(x1s, *ffn_w, tile=512).reshape(bs, ss, e)

    def st(conv, c, n, m):
        return conv[None], c[None], n[:, :, 0, :][None], m[:, 0, :N_HEADS][None]

    return (yp, ys) + st(convp, cp, np_, mp) + st(convs, cs, ns_, ms)
```

```python
import functools

import jax
import jax.numpy as jnp
from jax import lax
from jax.experimental import pallas as pl
from jax.experimental.pallas import tpu as pltpu

D_MODEL = 1024
N_HEADS = 4
HEAD_DIM = 256
CONV_WIDTH = 31
CONV_BUF = CONV_WIDTH - 1
D_FF = 4 * D_MODEL
DEPTH = 1
ALPHA = (2.0 * DEPTH) ** 0.25
LN_EPS = 1e-5

LANES = 128
SUBLANES = 8
MXU_DIM = 256
HIST_ROWS = 32
CONV_ROW_BLOCK = 256
MASKED = -1e30
VMEM_LIMIT_BYTES = 58 * 1024 * 1024

F32 = jnp.float32
BF16 = jnp.bfloat16


def _mm(a, b):
    return jnp.dot(a, b, preferred_element_type=F32)


def _mm_nt(a, b):
    return lax.dot_general(a, b, (((1,), (1,)), ((), ())), preferred_element_type=F32)


def _mm_tn(a, b):
    return lax.dot_general(a, b, (((0,), (0,)), ((), ())), preferred_element_type=F32)


def _bf16_terms(x):
    hi = x.astype(BF16)
    r1 = x - hi.astype(F32)
    mid = r1.astype(BF16)
    lo = (r1 - mid.astype(F32)).astype(BF16)
    return hi, mid, lo


def _ones_mm(ones, x):
    ob = ones.astype(BF16)
    hi, mid, lo = _bf16_terms(x)
    return _mm(ob, hi) + _mm(ob, mid) + _mm(ob, lo)


def _mm_ones(x, ones):
    ob = ones.astype(BF16)
    hi, mid, lo = _bf16_terms(x)
    return _mm(hi, ob) + _mm(mid, ob) + _mm(lo, ob)


PACK_ROWS = 1024
PACK_COLS = 512


def _pack_kernel(w_ref, o_ref):
    o_ref[...] = pltpu.bitcast(w_ref[...].astype(BF16), jnp.uint32)


def _pack_rows(w, cols=None, col_block=PACK_COLS, col_map=lambda j: j):
    if w.ndim == 2:
        w = w[None]
    _, k, n = w.shape
    cols = n if cols is None else cols
    kb = min(PACK_ROWS, k)
    return pl.pallas_call(
        _pack_kernel,
        grid=(k // kb, cols // col_block),
        in_specs=[pl.BlockSpec((None, kb, col_block), lambda i, j: (0, i, col_map(j)))],
        out_specs=pl.BlockSpec((kb // 2, col_block), lambda i, j: (i, j)),
        out_shape=jax.ShapeDtypeStruct((k // 2, cols), jnp.uint32),
        compiler_params=pltpu.CompilerParams(dimension_semantics=("arbitrary", "arbitrary")),
        name="pack_weight",
    )(w)


def _pack_t_kernel(wt_ref, o_ref):
    o_ref[...] = pltpu.bitcast(wt_ref[...].T.astype(BF16), jnp.uint32)


def _pack_rows_t(wt, cols=None, col_block=PACK_COLS, col_map=lambda j: j, row0=0):
    n, k = wt.shape
    cols = n if cols is None else cols
    if row0:
        in_spec = pl.BlockSpec((pl.Element(col_block), pl.Element(k)),
                               lambda j: (pl.multiple_of(row0 + col_map(j) * col_block, SUBLANES), 0))
    else:
        in_spec = pl.BlockSpec((col_block, k), lambda j: (col_map(j), 0))
    return pl.pallas_call(
        _pack_t_kernel,
        grid=(cols // col_block,),
        in_specs=[in_spec],
        out_specs=pl.BlockSpec((k // 2, col_block), lambda j: (0, j)),
        out_shape=jax.ShapeDtypeStruct((k // 2, cols), jnp.uint32),
        compiler_params=pltpu.CompilerParams(dimension_semantics=("arbitrary",)),
        name="pack_weight_t",
    )(wt)


def _w(ref, cols=slice(None)):
    return pltpu.bitcast(ref[:, cols], BF16)


def _sigmoid(x):
    return 0.5 * (jnp.tanh(0.5 * x) + 1.0)


def _log_sigmoid(x):
    return jnp.minimum(x, 0.0) - jnp.log(1.0 + jnp.exp(-jnp.abs(x)))


def _layer_norm(x, g, b):
    mu = jnp.mean(x, axis=-1, keepdims=True)
    xc = x - mu
    var = jnp.mean(xc * xc, axis=-1, keepdims=True)
    return xc * lax.rsqrt(var + LN_EPS) * g + b


def _mixer_kernel(ns, nc, L, nj,
                  x_ref, cache_ref, c_in, n_in, m_in,
                  wconv, wqkvo, wg, wgc, wgri, wgrf, bgc, bgr, wdw, bdw, lnag, lnab,
                  waout, hng, wbout, wout, ln1g, ln1b,
                  x1_ref, conv_out, c_out, n_out, m_out,
                  xb_scr, cb0, cb1, cb2, cb3, dw_scr, yain_scr, zq_scr, qkv_scr, zg_scr, ybuf):
    j = pl.program_id(1)
    carried = nj > 1
    Lc = nc * L
    T = ns * Lc
    lo = HIST_ROWS - CONV_BUF
    quarter = D_MODEL // 4

    n_blk = D_MODEL // LANES
    per_q = quarter // LANES
    windows = (cb0, cb1, cb2, cb3)

    def window(b):
        return windows[b // per_q]

    def load_history():
        for s in range(ns):
            for b in range(n_blk):
                window(b)[s, b % per_q, lo:HIST_ROWS, :] = cache_ref[s, :, b * LANES:(b + 1) * LANES]

    if carried:
        @pl.when(j == 0)
        def _():
            c_out[...] = c_in[...]
            n_out[...] = n_in[...]
            m_out[...] = m_in[...]
            load_history()
    else:
        load_history()

    def zero_tile(tile):
        bits = pltpu.bitcast(tile, jnp.uint32)
        zero = lax.shift_right_logical(lax.shift_right_logical(bits, jnp.uint32(16)), jnp.uint32(16))
        return pltpu.bitcast(zero, F32)

    def zero_after(tile):
        return zero_tile(tile)[0:1]

    xb_scr[...] = x_ref[...].astype(BF16)

    def glu_quarter(i):
        cs = slice(i * quarter, (i + 1) * quarter)
        gs = slice(D_MODEL + i * quarter, D_MODEL + (i + 1) * quarter)
        xb = xb_scr[...]
        glu = _mm(xb, _w(wconv, cs)) * _sigmoid(_mm(xb, _w(wconv, gs)))
        for s in range(ns):
            for b in range(quarter // LANES):
                windows[i][s, b, HIST_ROWS:HIST_ROWS + Lc, :] = (
                    glu[s * Lc:(s + 1) * Lc, b * LANES:(b + 1) * LANES])
        return windows[i][ns - 1, per_q - 1,
                       HIST_ROWS + Lc - SUBLANES:HIST_ROWS + Lc, :]

    rb = min(CONV_ROW_BLOCK, Lc)

    def conv_block(i, after=None):
        cs = slice(i * LANES, (i + 1) * LANES)
        bias = bdw[:, cs] if after is None else bdw[:, cs] + zero_after(after)
        group = max(1, min(ns, CONV_ROW_BLOCK // rb))
        for s0 in range(0, ns, group):
            for r0 in range(0, Lc, rb):
                accs = [jnp.broadcast_to(bias, (rb, LANES))] * group
                for t in range(CONV_WIDTH):
                    wt = jnp.broadcast_to(wdw[t:t + 1, cs], (SUBLANES, LANES))
                    if t > 0:
                        wt = wt + zero_tile(accs[0][0:SUBLANES])
                    for g in range(group):
                        win = window(i)[s0 + g, i % per_q, lo + r0 + t:lo + r0 + t + rb, :]
                        accs[g] = accs[g] + (win.reshape(rb // SUBLANES, SUBLANES, LANES) * wt
                                             ).reshape(rb, LANES)
                for g in range(group):
                    dw_scr[(s0 + g) * Lc + r0:(s0 + g) * Lc + r0 + rb, cs] = accs[g]

    def conv_tail():
        for s in range(ns):
            for b in range(n_blk):
                tail = window(b)[s, b % per_q, HIST_ROWS + Lc - CONV_BUF:HIST_ROWS + Lc, :]
                conv_out[s, :, b * LANES:(b + 1) * LANES] = tail
                if carried:
                    window(b)[s, b % per_q, lo:HIST_ROWS, :] = tail

    def ln_swish(r0, r1, after):
        y = _layer_norm(dw_scr[r0:r1, :], lnag[...], lnab[...] + zero_after(after)[:, 0:1])
        yain_scr[r0:r1, :] = (y * _sigmoid(y)).astype(BF16)

    def gate_activations():
        xb = xb_scr[...]
        zgc = _mm(xb, _w(wgc)) + bgc[...]
        ig_col = zgc[:, :LANES]
        lf_col = _log_sigmoid(zgc[:, LANES:])
        ig_row = _mm_nt(wgri[...], xb) + bgr[:, 0:1]
        lf_row = _log_sigmoid(_mm_nt(wgrf[...], xb) + bgr[:, 1:2])
        return ig_col, lf_col, ig_row, lf_row

    def gate_cumsums(ig_col, lf_col, ig_row, lf_row):
        rows = lax.broadcasted_iota(jnp.int32, (L, L), 0)
        cols = lax.broadcasted_iota(jnp.int32, (L, L), 1)
        causal = rows >= cols
        tri_l = causal.astype(F32)
        tri_u = (rows <= cols).astype(F32)
        per_chunk = []
        for ci in range(ns * nc):
            sl = slice(ci * L, (ci + 1) * L)
            b_col = _ones_mm(tri_l, lf_col[sl])
            b_row = _mm_ones(lf_row[:, sl], tri_u)
            per_chunk.append((ig_col[sl], ig_row[:, sl], b_col, b_row))
        return causal, per_chunk

    def qkvo_head(h):
        hs = slice(h * D_MODEL, (h + 1) * D_MODEL)
        z = _mm(xb_scr[...], _w(wqkvo, hs))
        qkv_scr[:, 3 * h * HEAD_DIM:(3 * h + 1) * HEAD_DIM] = z[:, :HEAD_DIM].astype(BF16)
        qkv_scr[:, (3 * h + 1) * HEAD_DIM:(3 * h + 2) * HEAD_DIM] = (
            z[:, HEAD_DIM:2 * HEAD_DIM] * scale).astype(BF16)
        qkv_scr[:, (3 * h + 2) * HEAD_DIM:(3 * h + 3) * HEAD_DIM] = z[:, 2 * HEAD_DIM:3 * HEAD_DIM].astype(BF16)
        zq_scr[:, h * HEAD_DIM:(h + 1) * HEAD_DIM] = z[:, 3 * HEAD_DIM:]
        return zq_scr[T - SUBLANES:T, (h + 1) * HEAD_DIM - LANES:(h + 1) * HEAD_DIM]

    lane = lax.broadcasted_iota(jnp.int32, (1, LANES), 1)
    scale = HEAD_DIM ** -0.5

    def mlstm_head(s, c, h, causal, chunk_gates, after):
        ig_col, ig_row, b_col, b_row = chunk_gates
        sl = slice((s * nc + c) * L, (s * nc + c + 1) * L)
        first = c == 0
        c_prev, n_prev, m_prev = ((c_out, n_out, m_out) if carried or not first
                                  else (c_in, n_in, m_in))
        b_c = b_col[:, h:h + 1]
        if ns == 1:
            b_c = b_c + zero_after(after)[:, 0:1]
        ig_c = ig_col[:, h:h + 1]
        b_r = b_row[h:h + 1, :]
        ig_r = ig_row[h:h + 1, :]
        m0 = m_prev[s][:, h:h + 1]
        a_c = b_c + m0
        dm = jnp.where(causal, b_c + (ig_r - b_r), MASKED)
        m_c = jnp.maximum(a_c, jnp.max(dm, axis=1, keepdims=True))
        w_intra = jnp.exp(dm - m_c)
        w_inter = jnp.exp(a_c - m_c)

        qb = qkv_scr[sl, 3 * h * HEAD_DIM:(3 * h + 1) * HEAD_DIM]
        kb = qkv_scr[sl, (3 * h + 1) * HEAD_DIM:(3 * h + 2) * HEAD_DIM]
        vb = qkv_scr[sl, (3 * h + 2) * HEAD_DIM:(3 * h + 3) * HEAD_DIM]
        o = zq_scr[sl, h * HEAD_DIM:(h + 1) * HEAD_DIM]
        q = qb.astype(F32)
        k = kb.astype(F32)
        c0 = c_prev[s, h]
        n0 = n_prev[s, h]

        smat = _mm_nt(qb, kb) * w_intra
        num = _mm(smat.astype(BF16), vb) + w_inter * _mm(qb, c0.astype(BF16))
        den = (jnp.sum(smat, axis=1, keepdims=True)
               + w_inter * jnp.sum(q * n0, axis=1, keepdims=True))
        hh = num / jnp.maximum(jnp.abs(den), jnp.exp(-m_c))

        m_end = m_c[L - 1:L]
        w_end = jnp.exp(b_c[L - 1:L] - b_c + ig_c - m_end)
        decay = jnp.exp(a_c[L - 1:L] - m_end)
        kw = k * w_end
        c_out[s, h] = decay * c0 + _mm_tn(kw.astype(BF16), vb)
        n_out[s, h] = decay * n0 + jnp.sum(kw, axis=0, keepdims=True)
        m_out[s] = jnp.where(lane == h, m_end, m_out[s] if (carried or not first or h > 0)
                             else m_in[s])

        mu = jnp.mean(hh, axis=1, keepdims=True)
        hc = hh - mu
        var = jnp.mean(hc * hc, axis=1, keepdims=True)
        hn = hc * lax.rsqrt(var + LN_EPS) * hng[:, h * HEAD_DIM:(h + 1) * HEAD_DIM]
        ybuf[sl, h * HEAD_DIM:(h + 1) * HEAD_DIM] = (_sigmoid(o) * hn).astype(BF16)

    def merge_gates_quarter(i):
        cs = slice(i * quarter, (i + 1) * quarter)
        gs = slice(D_MODEL + i * quarter, D_MODEL + (i + 1) * quarter)
        xb = xb_scr[...]
        zg_scr[:, cs] = _sigmoid(_mm(xb, _w(wg, cs)))
        zg_scr[:, gs] = _sigmoid(_mm(xb, _w(wg, gs)))
        return zg_scr[T - SUBLANES:T, D_MODEL + (i + 1) * quarter - LANES:D_MODEL + (i + 1) * quarter]

    a0 = glu_quarter(0)
    conv_block(0)
    conv_block(1)
    gate_act = gate_activations()
    a1 = glu_quarter(1)
    conv_block(2)
    conv_block(3)
    causal, chunk_gates = gate_cumsums(*gate_act)
    a2 = glu_quarter(2)
    conv_block(4)
    a3 = glu_quarter(3)
    conv_block(5, after=a1)
    b0 = qkvo_head(0)
    conv_block(6, after=a2)
    b1 = qkvo_head(1)
    conv_block(7, after=b0)
    conv_tail()
    b2 = qkvo_head(2)

    def heads(h, after):
        for s in range(ns):
            for c in range(nc):
                mlstm_head(s, c, h, causal, chunk_gates[s * nc + c], after)

    half = T // 2
    heads(0, b1)
    ln_swish(0, half, b2)
    b3 = qkvo_head(3)
    heads(1, b2)
    ln_swish(half, T, b3)
    merge_gates_quarter(0)
    c1 = merge_gates_quarter(1)
    heads(2, b3)
    merge_gates_quarter(2)
    merge_gates_quarter(3)
    ya_gated = zg_scr[:, :D_MODEL] * _mm(yain_scr[...], _w(waout))
    heads(3, c1)

    y_b = _mm(ybuf[...], _w(wbout))
    merged = (ya_gated + zg_scr[:, D_MODEL:] * y_b).astype(BF16)
    res = ALPHA * x_ref[...] + _mm(merged, _w(wout))
    x1_ref[...] = _layer_norm(res, ln1g[...], ln1b[...])


def _ffn_kernel(x1_ref, wff1, wff2, ln2g, ln2b, out_ref):
    rows = x1_ref.shape[0] // 2
    hid = []
    for r in range(2):
        x1 = x1_ref[r * rows:(r + 1) * rows, :]
        h = jnp.maximum(_mm(x1.astype(BF16), _w(wff1)), 0.0)
        hid.append((h * h).astype(BF16))
    for r in range(2):
        x1 = x1_ref[r * rows:(r + 1) * rows, :]
        ff = _mm(hid[r], _w(wff2))
        out_ref[r * rows:(r + 1) * rows, :] = _layer_norm(ALPHA * x1 + ff, ln2g[...], ln2b[...])


def _resident(shape):
    nd = len(shape)
    return pl.BlockSpec(shape, lambda *_: (0,) * nd, pipeline_mode=pl.Buffered(1))


def _mixer(x2d, cache, c0, n0, m0, weights, *, ns, nc, L, nj):
    Lc = nc * L
    T = ns * Lc
    n_tok = x2d.shape[0]
    G = n_tok // (T * nj)
    n_seq = G * ns
    state_specs = [
        pl.BlockSpec((ns, CONV_BUF, D_MODEL), lambda g, j: (g, 0, 0)),
        pl.BlockSpec((ns, N_HEADS, HEAD_DIM, HEAD_DIM), lambda g, j: (g, 0, 0, 0)),
        pl.BlockSpec((ns, N_HEADS, 1, HEAD_DIM), lambda g, j: (g, 0, 0, 0)),
        pl.BlockSpec((ns, 1, LANES), lambda g, j: (g, 0, 0)),
    ]
    state_in_specs = state_specs
    x_spec = pl.BlockSpec((T, D_MODEL), lambda g, j: (g * nj + j, 0))
    return pl.pallas_call(
        functools.partial(_mixer_kernel, ns, nc, L, nj),
        grid=(G, nj),
        in_specs=[x_spec] + state_in_specs + [_resident(w.shape) for w in weights],
        out_specs=[x_spec] + state_specs,
        out_shape=[
            jax.ShapeDtypeStruct((n_tok, D_MODEL), F32),
            jax.ShapeDtypeStruct((n_seq, CONV_BUF, D_MODEL), F32),
            jax.ShapeDtypeStruct((n_seq, N_HEADS, HEAD_DIM, HEAD_DIM), F32),
            jax.ShapeDtypeStruct((n_seq, N_HEADS, 1, HEAD_DIM), F32),
            jax.ShapeDtypeStruct((n_seq, 1, LANES), F32),
        ],
        scratch_shapes=[
            pltpu.VMEM((T, D_MODEL), BF16),
            *[pltpu.VMEM((ns, D_MODEL // 4 // LANES, HIST_ROWS + Lc, LANES), F32)
              for _ in range(4)],
            pltpu.VMEM((T, D_MODEL), F32),
            pltpu.VMEM((T, D_MODEL), BF16),
            pltpu.VMEM((T, D_MODEL), F32),
            pltpu.VMEM((T, 3 * D_MODEL), BF16),
            pltpu.VMEM((T, 2 * D_MODEL), F32),
            pltpu.VMEM((T, D_MODEL), BF16),
        ],
        compiler_params=pltpu.CompilerParams(
            dimension_semantics=("arbitrary", "arbitrary"),
            vmem_limit_bytes=VMEM_LIMIT_BYTES),
        name=f"mixer_ns{ns}",
    )(x2d, cache, c0, n0, m0, *weights)


def _ffn(x1, wff1, wff2, ln2g, ln2b, *, tile):
    n_tok = x1.shape[0]
    x_spec = pl.BlockSpec((tile, D_MODEL), lambda i: (i, 0))
    weights = (wff1, wff2, ln2g, ln2b)
    return pl.pallas_call(
        _ffn_kernel,
        grid=(n_tok // tile,),
        in_specs=[x_spec] + [_resident(w.shape) for w in weights],
        out_specs=x_spec,
        out_shape=jax.ShapeDtypeStruct((n_tok, D_MODEL), F32),
        compiler_params=pltpu.CompilerParams(
            dimension_semantics=("arbitrary",),
            vmem_limit_bytes=VMEM_LIMIT_BYTES),
        name="ffn",
    )(x1, *weights)


def _pad_m(m):
    return jnp.pad(m.astype(F32), ((0, 0), (0, LANES - N_HEADS)))[:, None, :]


def kernel(x_prompt, x_sample, cache_conv, state_C, state_n, state_m, w_in, b_gate, w_dw, b_dw,
           ln_a_g, ln_a_b, w_a_out, hn_g, w_b_out, w_out, ln1_g, ln1_b, w_ff1, w_ff2, ln2_g, ln2_b):
    assert w_in.shape[0] == DEPTH == 1
    bp, sp, _ = x_prompt.shape
    bs, ss, _ = x_sample.shape
    e = D_MODEL
    o_qkvo = 2 * e
    o_gate = o_qkvo + 4 * e
    o_ga = o_gate + 2 * N_HEADS

    def row(v):
        return v[0].astype(F32)[None, :]

    qkvo_blk0 = o_qkvo // HEAD_DIM

    def qkvo_block(j):
        return qkvo_blk0 + (j % 4) * N_HEADS + j // 4

    wt = jnp.swapaxes(w_in, 1, 2)[0]
    w_i_t = wt[o_gate:o_gate + N_HEADS]
    w_f_t = wt[o_gate + N_HEADS:o_ga]
    zpad = jnp.zeros((e, LANES - N_HEADS), wt.dtype)
    wgc = jnp.concatenate([w_i_t.T, zpad, w_f_t.T, zpad], axis=1)
    rpad = jnp.zeros((SUBLANES - N_HEADS, e), wt.dtype)
    wgri = jnp.concatenate([w_i_t, rpad], axis=0).astype(BF16)
    wgrf = jnp.concatenate([w_f_t, rpad], axis=0).astype(BF16)
    bg = b_gate[0].astype(F32)
    bpad = jnp.zeros((LANES - N_HEADS,), F32)
    bgc = jnp.concatenate([bg[:N_HEADS], bpad, bg[N_HEADS:], bpad])[None, :]
    bgr = jnp.pad(bg.reshape(2, N_HEADS).T, ((0, SUBLANES - N_HEADS), (0, 0)))

    weights = (
        _pack_rows_t(wt, cols=o_qkvo),
        _pack_rows_t(wt, cols=4 * e, col_block=HEAD_DIM, col_map=qkvo_block),
        _pack_rows_t(wt, cols=2 * e, row0=o_ga),
        _pack_rows(wgc, col_block=2 * LANES), wgri, wgrf, bgc, bgr,
        w_dw[0].astype(F32), row(b_dw), row(ln_a_g), row(ln_a_b),
        _pack_rows(w_a_out), row(hn_g), _pack_rows(w_b_out), _pack_rows(w_out),
        row(ln1_g), row(ln1_b),
    )
    ffn_w = (_pack_rows(w_ff1), _pack_rows(w_ff2), row(ln2_g), row(ln2_b))

    Lp = MXU_DIM
    ncp = 1
    x1p, convp, cp, np_, mp = _mixer(
        x_prompt.reshape(bp * sp, e),
        jnp.zeros((bp, CONV_BUF, e), F32),
        jnp.zeros((bp, N_HEADS, HEAD_DIM, HEAD_DIM), F32),
        jnp.zeros((bp, N_HEADS, 1, HEAD_DIM), F32),
        jnp.zeros((bp, 1, LANES), F32),
        weights, ns=1, nc=ncp, L=Lp, nj=sp // (ncp * Lp))
    x1s, convs, cs, ns_, ms = _mixer(
        x_sample.reshape(bs * ss, e),
        cache_conv[0].astype(F32),
        state_C[0].astype(F32),
        state_n[0].astype(F32)[:, :, None, :],
        _pad_m(state_m[0]),
        weights, ns=4, nc=1, L=ss, nj=1)

    yp = _ffn(x1p, *ffn_w, tile=512).reshape(bp, sp, e)
    ys = _ffn(x1s, *ffn_w, tile=512).reshape(bs, ss, e)

    def st(conv, c, n, m):
        return conv[None], c[None], n[:, :, 0, :][None], m[:, 0, :N_HEADS][None]

    return (yp, ys) + st(convp, cp, np_, mp) + st(convs, cs, ns_, ms)
```
